```python
import math
import jax, jax.numpy as jnp
from jax import lax
import numpy as np

D_MODEL = 2048
BATCH = 4
SEQ = 2048
DEPTH = 4
DEC_BATCH = 8
DEC_SEQ = 8
PAST_LEN = 16384
PAGE_SIZE = 128

SB_HEAD_DIM = 128
SB_HEADS = D_MODEL // (2 * SB_HEAD_DIM)
SB_WIDTH = SB_HEADS * SB_HEAD_DIM
Q_BLOCK = 128
SB_BIAS_INIT = -7.0
RW_HEAD_DIM = 64
RW_WIDTH = D_MODEL // 2
RW_HEADS = RW_WIDTH // RW_HEAD_DIM
LORA_W = max(32, int(round(1.8 * math.sqrt(RW_WIDTH) / 32)) * 32)
LORA_A = LORA_W
LORA_G = max(32, int(round(0.6 * RW_WIDTH ** 0.8 / 32)) * 32)
RW_IN = 3 * RW_WIDTH + LORA_W + LORA_A + LORA_G
RW_SPLITS = (RW_WIDTH, 2 * RW_WIDTH, 3 * RW_WIDTH, 3 * RW_WIDTH + LORA_W,
             3 * RW_WIDTH + LORA_W + LORA_A)
GN_EPS = 64e-5
EVEN_IN = 3 * SB_WIDTH + RW_IN
CHUNK = 128
C_WIDTH = 2 * D_MODEL
C_GROUPS = 8
C_GROUP_DIM = C_WIDTH // C_GROUPS
FFN_DIM = 2 * D_MODEL
CONV_W = 3
N_EVEN = (DEPTH + 1) // 2
N_ODD = DEPTH // 2
RMS_EPS = 1e-6
LN_EPS = 1e-5

kernel_name = "sb_rwkv7_gmlp_hybrid_decode_step"


def rmsnorm(x, g):
    xf = x.astype(jnp.float32)
    y = xf * lax.rsqrt(jnp.mean(xf * xf, axis=-1, keepdims=True) + RMS_EPS) * g.astype(jnp.float32)
    return y.astype(x.dtype)


def layernorm(x, w, b):
    xf = x.astype(jnp.float32)
    mu = jnp.mean(xf, axis=-1, keepdims=True)
    var = jnp.mean(jnp.square(xf - mu), axis=-1, keepdims=True)
    return ((xf - mu) * lax.rsqrt(var + LN_EPS) * w + b).astype(x.dtype)


def stick_breaking(q, k, v, bias, q_pos, k_pos):
    f32 = jnp.float32
    z = (jnp.einsum('bqhd,bkhd->bhqk', q.astype(f32), k.astype(f32)) * (SB_HEAD_DIM ** -0.5)
         + bias.astype(f32)[None, :, None, None])
    mask = k_pos[None, :] < q_pos[:, None]
    log_stay = jnp.where(mask, jax.nn.log_sigmoid(-z), 0.0)
    log_between = lax.cumsum(log_stay, axis=3, reverse=True) - log_stay
    att = jnp.where(mask, jnp.exp(jax.nn.log_sigmoid(z) + log_between), 0.0)
    return jnp.einsum('bhqk,bkhd->bqhd', att, v.astype(f32))


def sb_prompt(q, k, v, bias):
    Bn, S, H, Dh = q.shape
    nb = S // Q_BLOCK
    qb = jnp.swapaxes(q.reshape(Bn, nb, Q_BLOCK, H, Dh), 0, 1)
    k_pos = jnp.arange(S)

    def one_block(args):
        q_blk, i = args
        q_pos = i * Q_BLOCK + jnp.arange(Q_BLOCK)
        return stick_breaking(q_blk, k, v, bias, q_pos, k_pos)

    ob = lax.map(one_block, (qb, jnp.arange(nb)))
    return jnp.swapaxes(ob, 0, 1).reshape(Bn, S, H, Dh)


def sb_sample(q, k_new, v_new, bias, cache_k, cache_v, layer, page_table):
    Bd, T, H, Dh = q.shape
    past = page_table.shape[1] * PAGE_SIZE
    k_past = cache_k[layer][page_table].reshape(Bd, past, H, Dh)
    v_past = cache_v[layer][page_table].reshape(Bd, past, H, Dh)
    k_all = jnp.concatenate([k_past, k_new.astype(k_past.dtype)], axis=1)
    v_all = jnp.concatenate([v_past, v_new.astype(v_past.dtype)], axis=1)
    q_pos = past + jnp.arange(T)
    k_pos = jnp.arange(past + T)
    return stick_breaking(q, k_all, v_all, bias, q_pos, k_pos)


def rwkv7_mix(zb, z_prev, wkv0, mu, w0, w2, a0, a2, g2, k_k, k_a, r_k, lnx_w, lnx_b):
    f32 = jnp.float32
    Bn, T, _ = zb.shape
    shifted = jnp.concatenate([z_prev[:, None, :].astype(zb.dtype), zb[:, :-1]], axis=1)
    xz = zb + (shifted - zb) * mu
    r, k, v, zw, za, zg = jnp.split(xz, RW_SPLITS, axis=-1)
    w_log = -jax.nn.softplus(-(w0 + jnp.tanh(zw) @ w2).astype(f32)) - 0.5
    decay = jnp.exp(-jnp.exp(w_log))
    a = jax.nn.sigmoid((a0 + za @ a2).astype(f32))
    g = jax.nn.sigmoid(zg) @ g2
    k = k.astype(f32)

    def heads(t):
        return t.astype(f32).reshape(Bn, T, RW_HEADS, RW_HEAD_DIM)

    kk = heads(k * k_k)
    kk = kk / jnp.maximum(jnp.sqrt(jnp.sum(kk * kk, axis=-1, keepdims=True)), 1e-12)
    k_mod = heads(k * (1.0 + (a - 1.0) * k_a))
    r_h, v_h, a_h, w_h = heads(r), heads(v), heads(a), heads(decay)

    def step(S, inp):
        r_t, k_t, v_t, kk_t, b_t, w_t = inp
        sa = jnp.einsum('bhvk,bhk->bhv', S, -kk_t)
        S = (S * w_t[:, :, None, :] + sa[..., None] * b_t[:, :, None, :]
             + v_t[..., None] * k_t[:, :, None, :])
        return S, jnp.einsum('bhvk,bhk->bhv', S, r_t)

    tm = lambda t: jnp.swapaxes(t, 0, 1)
    S_fin, o = lax.scan(step, wkv0.astype(f32),
                        (tm(r_h), tm(k_mod), tm(v_h), tm(kk), tm(kk * a_h), tm(w_h)))
    o = tm(o)
    o_mu = jnp.mean(o, axis=-1, keepdims=True)
    o_var = jnp.mean(jnp.square(o - o_mu), axis=-1, keepdims=True)
    o = ((o - o_mu) * lax.rsqrt(o_var + GN_EPS)).reshape(Bn, T, RW_WIDTH) * lnx_w + lnx_b
    bonus = jnp.sum(r_h * k_mod * r_k, axis=-1, keepdims=True) * v_h
    out = (o + bonus.reshape(Bn, T, RW_WIDTH)) * g
    return out.astype(zb.dtype), zb[:, -1], S_fin.astype(wkv0.dtype)


def chunk_gmlp(h, n_chunks, w_in, ln_w, ln_b, w_s, b_s, w_out):
    Bn, T, _ = h.shape
    L = T // n_chunks
    z = jax.nn.gelu(h @ w_in)
    u, v = jnp.split(z, 2, axis=-1)
    v = layernorm(v, ln_w, ln_b)
    causal = jnp.tril(jnp.ones((L, L), dtype=bool))
    ws = jnp.where(causal, w_s[:, :L, :L], 0.0)
    vg = v.reshape(Bn, n_chunks, L, C_GROUPS, C_GROUP_DIM)
    mixed = jnp.einsum('gts,bcsgd->bctgd', ws, vg) + b_s[:, :L].T[None, None, :, :, None]
    out = u * mixed.reshape(Bn, T, C_WIDTH)
    return (out @ w_out).astype(h.dtype), v


def conv_ffn(h, w_up, conv_w, conv_b, w_down, conv_prev):
    T = h.shape[1]
    up = h @ w_up
    ext = jnp.concatenate([conv_prev.astype(up.dtype), up], axis=1)
    c = conv_b + sum(ext[:, j:j + T] * conv_w[j] for j in range(CONV_W))
    gate, val = jnp.split(c, 2, axis=-1)
    out = (jax.nn.silu(gate) * val) @ w_down
    return out.astype(h.dtype), ext[:, -(CONV_W - 1):]


def split_even(z):
    Bn, T, _ = z.shape
    qkv = z[..., :3 * SB_WIDTH].reshape(Bn, T, 3, SB_HEADS, SB_HEAD_DIM)
    return qkv[:, :, 0], qkv[:, :, 1], qkv[:, :, 2], z[..., 3 * SB_WIDTH:]


def setup_inputs(seed: int = 0) -> dict:
    key = jax.random.key(seed)
    keys = iter(jax.random.split(key, 40))
    f32 = jnp.float32

    def nrm(shape, scale):
        return jax.random.normal(next(keys), shape, f32) * scale

    def near_one(shape):
        return 1.0 + nrm(shape, 0.05)

    n_pages = PAST_LEN // PAGE_SIZE
    n_used = DEC_BATCH * n_pages
    n_pool = n_used + max(1, n_used // 4)
    x_prompt = nrm((BATCH, SEQ, D_MODEL), 1.0)
    x_sample = nrm((DEC_BATCH, DEC_SEQ, D_MODEL), 1.0)
    cache_k = nrm((N_EVEN, n_pool, PAGE_SIZE, SB_HEADS, SB_HEAD_DIM), 1.0)
    cache_v = nrm((N_EVEN, n_pool, PAGE_SIZE, SB_HEADS, SB_HEAD_DIM), 1.0)
    page_table = jax.random.permutation(next(keys), n_pool)[:n_used].reshape(
        DEC_BATCH, n_pages).astype(jnp.int32)
    state_shift = nrm((N_EVEN, DEC_BATCH, RW_IN), 1.0)
    state_wkv = nrm((N_EVEN, DEC_BATCH, RW_HEADS, RW_HEAD_DIM, RW_HEAD_DIM), 0.3)
    state_conv = nrm((DEPTH, DEC_BATCH, CONV_W - 1, 2 * FFN_DIM), 1.0)
    norm_mix = near_one((DEPTH, D_MODEL))
    norm_ffn = near_one((DEPTH, D_MODEL))
    norm_final = near_one((D_MODEL,))
    w_in_even = nrm((N_EVEN, D_MODEL, EVEN_IN), D_MODEL ** -0.5)
    sb_bias = SB_BIAS_INIT + nrm((N_EVEN, SB_HEADS), 0.5)
    mu_shift = jax.random.uniform(next(keys), (N_EVEN, RW_IN), f32)
    w0 = jax.random.uniform(next(keys), (N_EVEN, RW_WIDTH), f32, -6.0, -1.0)
    w2 = nrm((N_EVEN, LORA_W, RW_WIDTH), 0.5 * LORA_W ** -0.5)
    a0 = nrm((N_EVEN, RW_WIDTH), 0.1)
    a2 = nrm((N_EVEN, LORA_A, RW_WIDTH), 0.5 * LORA_A ** -0.5)
    g2 = nrm((N_EVEN, LORA_G, RW_WIDTH), LORA_G ** -0.5)
    k_k = 0.85 + nrm((N_EVEN, RW_WIDTH), 0.05)
    k_a = near_one((N_EVEN, RW_WIDTH))
    r_k = nrm((N_EVEN, RW_HEADS, RW_HEAD_DIM), 0.1)
    lnx_w = near_one((N_EVEN, RW_WIDTH))
    lnx_b = nrm((N_EVEN, RW_WIDTH), 0.01)
    w_out_even = nrm((N_EVEN, SB_WIDTH + RW_WIDTH, D_MODEL), 0.5 * (SB_WIDTH + RW_WIDTH) ** -0.5)
    w_in_odd = nrm((N_ODD, D_MODEL, 2 * C_WIDTH), D_MODEL ** -0.5)
    ln_v_w = near_one((N_ODD, C_WIDTH))
    ln_v_b = nrm((N_ODD, C_WIDTH), 0.01)
    w_spatial = nrm((N_ODD, C_GROUPS, CHUNK, CHUNK), 0.5 * CHUNK ** -0.5)
    b_spatial = 1.0 + nrm((N_ODD, C_GROUPS, CHUNK), 0.1)
    w_out_odd = nrm((N_ODD, C_WIDTH, D_MODEL), 0.5 * C_WIDTH ** -0.5)
    w_up = nrm((DEPTH, D_MODEL, 2 * FFN_DIM), D_MODEL ** -0.5)
    conv_w = nrm((DEPTH, CONV_W, 2 * FFN_DIM), CONV_W ** -0.5)
    conv_b = nrm((DEPTH, 2 * FFN_DIM), 0.01)
    w_down = nrm((DEPTH, FFN_DIM, D_MODEL), 0.5 * FFN_DIM ** -0.5)
    return {"x_prompt": x_prompt, "x_sample": x_sample, "cache_k": cache_k, "cache_v": cache_v,
            "page_table": page_table, "state_shift": state_shift, "state_wkv": state_wkv,
            "state_conv": state_conv, "norm_mix": norm_mix, "norm_ffn": norm_ffn,
            "norm_final": norm_final, "w_in_even": w_in_even, "sb_bias": sb_bias,
            "mu_shift": mu_shift, "w0": w0,
            "w2": w2, "a0": a0, "a2": a2, "g2": g2, "k_k": k_k, "k_a": k_a, "r_k": r_k,
            "lnx_w": lnx_w, "lnx_b": lnx_b, "w_out_even": w_out_even, "w_in_odd": w_in_odd,
            "ln_v_w": ln_v_w, "ln_v_b": ln_v_b, "w_spatial": w_spatial, "b_spatial": b_spatial,
            "w_out_odd": w_out_odd, "w_up": w_up, "conv_w": conv_w, "conv_b": conv_b,
            "w_down": w_down}


def reference(x_prompt, x_sample, cache_k, cache_v, page_table, state_shift, state_wkv,
              state_conv, norm_mix, norm_ffn, norm_final, w_in_even, sb_bias, mu_shift, w0, w2,
              a0, a2, g2, k_k, k_a, r_k, lnx_w, lnx_b, w_out_even, w_in_odd, ln_v_w, ln_v_b,
              w_spatial, b_spatial, w_out_odd, w_up, conv_w, conv_b, w_down):
    xp, xs = x_prompt, x_sample
    Bp, Sp = xp.shape[:2]
    Bs, Ts = xs.shape[:2]
    k_p, v_p, k_s, v_s = [], [], [], []
    sh_p, sh_s, wkv_p, wkv_s = [], [], [], []
    cv_p, cv_s, chunkv_s = [], [], []
    for l in range(DEPTH):
        hp = rmsnorm(xp, norm_mix[l])
        hs = rmsnorm(xs, norm_mix[l])
        i = l // 2
        if l % 2 == 0:
            qp, kp, vp, zbp = split_even(hp @ w_in_even[i])
            qs, ks, vs, zbs = split_even(hs @ w_in_even[i])
            att_p = sb_prompt(qp, kp, vp, sb_bias[i]).astype(xp.dtype).reshape(Bp, Sp, SB_WIDTH)
            att_s = sb_sample(qs, ks, vs, sb_bias[i], cache_k, cache_v, i, page_table).astype(
                xs.dtype).reshape(Bs, Ts, SB_WIDTH)
            rw = (mu_shift[i], w0[i], w2[i], a0[i], a2[i], g2[i], k_k[i], k_a[i], r_k[i],
                  lnx_w[i], lnx_b[i])
            rw_p, shift_p, S_p = rwkv7_mix(zbp, jnp.zeros((Bp, RW_IN), xp.dtype),
                                           jnp.zeros((Bp, RW_HEADS, RW_HEAD_DIM, RW_HEAD_DIM),
                                                     xp.dtype), *rw)
            rw_s, shift_s, S_s = rwkv7_mix(zbs, state_shift[i], state_wkv[i], *rw)
            mp = (jnp.concatenate([att_p, rw_p], axis=-1) @ w_out_even[i]).astype(xp.dtype)
            ms = (jnp.concatenate([att_s, rw_s], axis=-1) @ w_out_even[i]).astype(xs.dtype)
            k_p.append(kp)
            v_p.append(vp)
            k_s.append(ks)
            v_s.append(vs)
            sh_p.append(shift_p)
            sh_s.append(shift_s)
            wkv_p.append(S_p)
            wkv_s.append(S_s)
        else:
            mp, _ = chunk_gmlp(hp, Sp // CHUNK, w_in_odd[i], ln_v_w[i], ln_v_b[i],
                               w_spatial[i], b_spatial[i], w_out_odd[i])
            ms, v_rows = chunk_gmlp(hs, 1, w_in_odd[i], ln_v_w[i], ln_v_b[i],
                                    w_spatial[i], b_spatial[i], w_out_odd[i])
            chunkv_s.append(v_rows)
        xp = xp + mp
        xs = xs + ms
        fp, c_p = conv_ffn(rmsnorm(xp, norm_ffn[l]), w_up[l], conv_w[l], conv_b[l], w_down[l],
                           jnp.zeros((Bp, CONV_W - 1, 2 * FFN_DIM), xp.dtype))
        fs, c_s = conv_ffn(rmsnorm(xs, norm_ffn[l]), w_up[l], conv_w[l], conv_b[l], w_down[l],
                           state_conv[l])
        xp = xp + fp
        xs = xs + fs
        cv_p.append(c_p)
        cv_s.append(c_s)
    y_prompt = rmsnorm(xp, norm_final)
    y_sample = rmsnorm(xs, norm_final)
    return (y_prompt, y_sample, jnp.stack(k_p), jnp.stack(v_p), jnp.stack(k_s), jnp.stack(v_s),
            jnp.stack(sh_p), jnp.stack(sh_s), jnp.stack(wkv_p), jnp.stack(wkv_s),
            jnp.stack(cv_p), jnp.stack(cv_s), jnp.stack(chunkv_s))
```

```python
import functools

import jax
import jax.numpy as jnp
from jax import lax
from jax.experimental import pallas as pl
from jax.experimental.pallas import tpu as pltpu

F32 = jnp.float32
BF16 = jnp.bfloat16

D_MODEL = 2048
DEPTH = 4
PAGE_SIZE = 128
SB_HEAD_DIM = 128
SB_HEADS = 8
SB_WIDTH = 1024
RW_HEAD_DIM = 64
RW_WIDTH = 1024
RW_HEADS = 16
LORA_W = 64
LORA_A = 64
LORA_G = 160
RW_IN = 3 * RW_WIDTH + LORA_W + LORA_A + LORA_G
RW_PAD = 3 * RW_WIDTH + 128 + 128 + 256
GN_EPS = 64e-5
C_WIDTH = 2 * D_MODEL
C_GROUPS = 8
C_GROUP_DIM = C_WIDTH // C_GROUPS
CHUNK = 128
FFN_DIM = 2 * D_MODEL
CONV_W = 3
RMS_EPS = 1e-6
LN_EPS = 1e-5
SB_SCALE = SB_HEAD_DIM ** -0.5

VMEM_LIMIT_BYTES = 52 * 1024 * 1024


def _params(*sem):
    return pltpu.CompilerParams(dimension_semantics=sem, vmem_limit_bytes=VMEM_LIMIT_BYTES)


def _rmsnorm_body(x_ref, g_ref, o_ref):
    x = x_ref[...]
    ms = jnp.mean(x * x, axis=-1, keepdims=True)
    o_ref[...] = (x * lax.rsqrt(ms + RMS_EPS) * g_ref[...]).astype(o_ref.dtype)


def rmsnorm(x, g, out_dtype):
    m, d = x.shape
    tm = min(m, 512)
    return pl.pallas_call(
        _rmsnorm_body,
        grid=(m // tm,),
        in_specs=[pl.BlockSpec((tm, d), lambda i: (i, 0)),
                  pl.BlockSpec((1, d), lambda i: (0, 0))],
        out_specs=pl.BlockSpec((tm, d), lambda i: (i, 0)),
        out_shape=jax.ShapeDtypeStruct((m, d), out_dtype),
        compiler_params=_params("parallel"),
        name="rmsnorm",
    )(x, g.reshape(1, d))


def _matmul_body(*refs, act, has_res):
    if has_res:
        x_ref, w_ref, r_ref, o_ref = refs
    else:
        x_ref, w_ref, o_ref = refs
    acc = jnp.dot(x_ref[...], w_ref[...], preferred_element_type=F32)
    if act == "gelu":
        acc = jax.nn.gelu(acc)
    if has_res:
        acc = r_ref[...] + acc
    o_ref[...] = acc.astype(o_ref.dtype)


def matmul(x, w, res=None, act=None, out_dtype=F32):
    m, k = x.shape
    n = w.shape[1]
    tm = min(m, 1024)
    tn = 1024 if (k <= 2048 and n % 1024 == 0) else 512
    in_specs = [pl.BlockSpec((tm, k), lambda i, j: (i, 0)),
                pl.BlockSpec((k, tn), lambda i, j: (0, j))]
    args = [x, w]
    if res is not None:
        in_specs.append(pl.BlockSpec((tm, tn), lambda i, j: (i, j)))
        args.append(res)
    return pl.pallas_call(
        functools.partial(_matmul_body, act=act, has_res=res is not None),
        grid=(m // tm, n // tn),
        in_specs=in_specs,
        out_specs=pl.BlockSpec((tm, tn), lambda i, j: (i, j)),
        out_shape=jax.ShapeDtypeStruct((m, n), out_dtype),
        compiler_params=_params("parallel", "parallel"),
        name="matmul",
    )(*args)


def _tri_and_mask(n):
    row = lax.broadcasted_iota(jnp.int32, (n, n), 0)
    col = lax.broadcasted_iota(jnp.int32, (n, n), 1)
    tri = jnp.where(row > col, 1.0, 0.0).astype(BF16)
    return tri, row, col


def _sb_block(q, kb, vb, bias, carry, acc, tri, mask):
    z = lax.dot_general(q, kb, (((1,), (1,)), ((), ())), preferred_element_type=F32)
    z = z * SB_SCALE + bias
    sp = jnp.log1p(jnp.exp(-jnp.abs(z)))
    log_beta = jnp.minimum(z, 0.0) - sp
    log_stay = -jnp.maximum(z, 0.0) - sp
    if mask is not None:
        log_stay = jnp.where(mask, log_stay, 0.0)
    hi = log_stay.astype(BF16)
    lo = (log_stay - hi.astype(F32)).astype(BF16)
    between = (carry + jnp.dot(hi, tri, preferred_element_type=F32)
               + jnp.dot(lo, tri, preferred_element_type=F32))
    att = jnp.exp(log_beta + between)
    if mask is not None:
        att = jnp.where(mask, att, 0.0)
    acc = acc + jnp.dot(att.astype(BF16), vb, preferred_element_type=F32)
    carry = carry + jnp.sum(log_stay, axis=1, keepdims=True)
    return carry, acc


def _sb_prompt_body(bias_ref, q_ref, k_ref, v_ref, o_ref, *, tq):
    h = pl.program_id(1)
    i = pl.program_id(2)
    bias = bias_ref[h]
    q = q_ref[...].astype(BF16)
    tri, row, col = _tri_and_mask(tq)

    def blk(j, carry, acc, mask):
        off = pl.multiple_of(j * tq, tq)
        kb = k_ref[pl.ds(off, tq), :].astype(BF16)
        vb = v_ref[pl.ds(off, tq), :].astype(BF16)
        return _sb_block(q, kb, vb, bias, carry, acc, tri, mask)

    carry = jnp.zeros((tq, 1), F32)
    acc = jnp.zeros((tq, SB_HEAD_DIM), F32)
    carry, acc = blk(i, carry, acc, col < row)

    def body(n, c):
        return blk(i - 1 - n, c[0], c[1], None)

    carry, acc = lax.fori_loop(0, i, body, (carry, acc))
    o_ref[...] = acc.astype(o_ref.dtype)


def sb_prompt(qkv, bias, batch, seq):
    tq = 256
    nq = seq // tq
    return pl.pallas_call(
        functools.partial(_sb_prompt_body, tq=tq),
        grid=(batch, SB_HEADS, nq),
        in_specs=[pl.BlockSpec(memory_space=pltpu.SMEM),
                  pl.BlockSpec((tq, SB_HEAD_DIM), lambda b, h, i: (b * nq + i, h)),
                  pl.BlockSpec((seq, SB_HEAD_DIM), lambda b, h, i: (b, SB_HEADS + h)),
                  pl.BlockSpec((seq, SB_HEAD_DIM), lambda b, h, i: (b, 2 * SB_HEADS + h))],
        out_specs=pl.BlockSpec((tq, SB_HEAD_DIM), lambda b, h, i: (b * nq + i, h)),
        out_shape=jax.ShapeDtypeStruct((batch * seq, SB_WIDTH), BF16),
        compiler_params=_params("parallel", "parallel", "arbitrary"),
        name="sb_prompt",
    )(bias, qkv, qkv, qkv)


def _sb_sample_body(pt_ref, qbd_ref, bias_ref, kn_ref, vn_ref, kc_ref, vc_ref, o_ref,
                    carry_ref, acc_ref, *, t_new):
    j = pl.program_id(1)
    rows = qbd_ref.shape[0]
    q = qbd_ref[...]
    bias = bias_ref[...]
    tri, _, _ = _tri_and_mask(PAGE_SIZE)

    @pl.when(j == 0)
    def _():
        row = lax.broadcasted_iota(jnp.int32, (rows, PAGE_SIZE), 0)
        col = lax.broadcasted_iota(jnp.int32, (rows, PAGE_SIZE), 1)
        mask = col < (row % t_new)
        carry, acc = _sb_block(q, kn_ref[...].astype(BF16), vn_ref[...].astype(BF16), bias,
                               jnp.zeros((rows, 1), F32), jnp.zeros(acc_ref.shape, F32), tri, mask)
        carry_ref[...] = carry
        acc_ref[...] = acc

    @pl.when(j > 0)
    def _():
        carry, acc = _sb_block(q, kc_ref[...].astype(BF16), vc_ref[...].astype(BF16), bias,
                               carry_ref[...], acc_ref[...], tri, None)
        carry_ref[...] = carry
        acc_ref[...] = acc

    @pl.when(j == pl.num_programs(1) - 1)
    def _():
        o_ref[...] = acc_ref[...]


def sb_sample(q, k_new, v_new, bias, cache_k, cache_v, layer, page_table):
    bd, n_pages = page_table.shape
    t_new = q.shape[0] // bd
    rows = SB_HEADS * t_new
    n_pool = cache_k.shape[1]
    q4 = q.reshape(bd, t_new, SB_HEADS, SB_HEAD_DIM).transpose(0, 2, 1, 3)
    eye = jnp.eye(SB_HEADS, dtype=F32)
    qbd = (q4[:, :, :, None, :] * eye[None, :, None, :, None]).reshape(bd, rows, SB_WIDTH).astype(BF16)
    bias_rows = jnp.broadcast_to(jnp.repeat(bias, t_new)[:, None], (rows, PAGE_SIZE))
    pad = ((0, 0), (0, PAGE_SIZE - t_new), (0, 0))
    kn = jnp.pad(k_new.reshape(bd, t_new, SB_WIDTH), pad)
    vn = jnp.pad(v_new.reshape(bd, t_new, SB_WIDTH), pad)
    kc = cache_k.reshape(cache_k.shape[0], n_pool, PAGE_SIZE, SB_WIDTH)
    vc = cache_v.reshape(cache_v.shape[0], n_pool, PAGE_SIZE, SB_WIDTH)

    def page_map(b, j, pt):
        return (layer, pt[b * n_pages + n_pages - jnp.maximum(j, 1)], 0, 0)

    out = pl.pallas_call(
        functools.partial(_sb_sample_body, t_new=t_new),
        grid_spec=pltpu.PrefetchScalarGridSpec(
            num_scalar_prefetch=1,
            grid=(bd, n_pages + 1),
            in_specs=[pl.BlockSpec((None, rows, SB_WIDTH), lambda b, j, pt: (b, 0, 0)),
                      pl.BlockSpec((rows, PAGE_SIZE), lambda b, j, pt: (0, 0)),
                      pl.BlockSpec((None, PAGE_SIZE, SB_WIDTH), lambda b, j, pt: (b, 0, 0)),
                      pl.BlockSpec((None, PAGE_SIZE, SB_WIDTH), lambda b, j, pt: (b, 0, 0)),
                      pl.BlockSpec((None, None, PAGE_SIZE, SB_WIDTH), page_map),
                      pl.BlockSpec((None, None, PAGE_SIZE, SB_WIDTH), page_map)],
            out_specs=pl.BlockSpec((None, rows, SB_WIDTH), lambda b, j, pt: (b, 0, 0)),
            scratch_shapes=[pltpu.VMEM((rows, 1), F32), pltpu.VMEM((rows, SB_WIDTH), F32)]),
        out_shape=jax.ShapeDtypeStruct((bd, rows, SB_WIDTH), F32),
        compiler_params=_params("parallel", "arbitrary"),
        name="sb_sample",
    )(page_table.reshape(-1), qbd, bias_rows, kn, vn, kc, vc)
    o5 = out.reshape(bd, SB_HEADS, t_new, SB_HEADS, SB_HEAD_DIM)
    idx = jnp.arange(SB_HEADS)
    diag = o5[:, idx, :, idx, :]
    return diag.transpose(1, 2, 0, 3).reshape(bd * t_new, SB_WIDTH).astype(BF16)


def _split_bf16(x):
    hi = x.astype(BF16)
    lo = (x - hi.astype(F32)).astype(BF16)
    return hi, lo


def _dot3(a, b_hi, b_lo):
    a_hi, a_lo = _split_bf16(a)
    return (jnp.dot(a_hi, b_hi, preferred_element_type=F32)
            + jnp.dot(a_hi, b_lo, preferred_element_type=F32)
            + jnp.dot(a_lo, b_hi, preferred_element_type=F32))


def _head_sum(x, ones_bd):
    hi, lo = _split_bf16(x)
    return (jnp.dot(hi, ones_bd, preferred_element_type=F32)
            + jnp.dot(lo, ones_bd, preferred_element_type=F32))


def _rw_prep_body(zb_ref, halo_ref, prev_ref, mu_ref, w0_ref, a0_ref, kk_ref_w, ka_ref, rk_ref,
                  w2h_ref, w2l_ref, a2h_ref, a2l_ref, g2h_ref, g2l_ref, ones_ref,
                  r_out, w_out, km_out, kk_out, b_out, v_out, g_out, bonus_out):
    i = pl.program_id(1)
    zb = zb_ref[...]
    prev_row = jnp.where(i == 0, prev_ref[...], halo_ref[7:8, :])
    row = lax.broadcasted_iota(jnp.int32, zb.shape, 0)
    shifted = jnp.where(row == 0, prev_row, pltpu.roll(zb, 1, axis=0))
    xz = zb + (shifted - zb) * mu_ref[...]
    c = RW_WIDTH
    r = xz[:, 0:c]
    k = xz[:, c:2 * c]
    v = xz[:, 2 * c:3 * c]
    zw = xz[:, 3 * c:3 * c + 128]
    za = xz[:, 3 * c + 128:3 * c + 256]
    zg = xz[:, 3 * c + 256:3 * c + 512]
    ones_bd = ones_ref[...]
    w_log = -jax.nn.softplus(-(w0_ref[...] + _dot3(jnp.tanh(zw), w2h_ref[...], w2l_ref[...]))) - 0.5
    decay = jnp.exp(-jnp.exp(w_log))
    a = jax.nn.sigmoid(a0_ref[...] + _dot3(za, a2h_ref[...], a2l_ref[...]))
    g = _dot3(jax.nn.sigmoid(zg), g2h_ref[...], g2l_ref[...])
    kk = k * kk_ref_w[...]
    kk = kk / jnp.maximum(jnp.sqrt(_head_sum(kk * kk, ones_bd)), 1e-12)
    k_mod = k * (1.0 + (a - 1.0) * ka_ref[...])
    bonus = _head_sum(r * k_mod * rk_ref[...], ones_bd) * v
    r_out[...] = r
    w_out[...] = decay
    km_out[...] = k_mod
    kk_out[...] = kk
    b_out[...] = kk * a
    v_out[...] = v
    g_out[...] = g
    bonus_out[...] = bonus


def rw_prep(zb, z_prev, p, batch, seq):
    tt = min(seq, 256)
    nt = seq // tt
    rows = batch * seq
    hb = tt // 8

    def row_spec(width):
        return pl.BlockSpec((tt, width), lambda b, i: (b * nt + i, 0))

    def const_spec(shape):
        return pl.BlockSpec(shape, lambda b, i: (0,) * len(shape))

    out_sds = jax.ShapeDtypeStruct((rows, RW_WIDTH), F32)
    return pl.pallas_call(
        _rw_prep_body,
        grid=(batch, nt),
        in_specs=[row_spec(RW_PAD),
                  pl.BlockSpec((8, RW_PAD), lambda b, i: (jnp.maximum((b * nt + i) * hb - 1, 0), 0)),
                  pl.BlockSpec((None, 1, RW_PAD), lambda b, i: (b, 0, 0)),
                  const_spec((1, RW_PAD)),
                  const_spec((1, RW_WIDTH)), const_spec((1, RW_WIDTH)), const_spec((1, RW_WIDTH)),
                  const_spec((1, RW_WIDTH)), const_spec((1, RW_WIDTH)),
                  const_spec((128, RW_WIDTH)), const_spec((128, RW_WIDTH)),
                  const_spec((128, RW_WIDTH)), const_spec((128, RW_WIDTH)),
                  const_spec((256, RW_WIDTH)), const_spec((256, RW_WIDTH)),
                  const_spec((RW_WIDTH, RW_WIDTH))],
        out_specs=[row_spec(RW_WIDTH)] * 8,
        out_shape=[out_sds] * 8,
        compiler_params=_params("parallel", "arbitrary"),
        name="rw_prep",
    )(zb, zb, z_prev.reshape(batch, 1, RW_PAD), p["mu"], p["w0"], p["a0"], p["k_k"], p["k_a"], p["r_k"],
      p["w2h"], p["w2l"], p["a2h"], p["a2l"], p["g2h"], p["g2l"], p["ones_bd"])


def _rw_scan_body(kk_ref, w_ref, b_ref, km_ref, r_ref, v_ref, s0_ref, o_ref, s_ref, *, nv, tc):
    @pl.when(pl.program_id(0) == 0)
    def _():
        s_ref[...] = s0_ref[...]

    def step(t, _):
        kk = kk_ref[t]
        w = w_ref[t]
        bm = b_ref[t]
        km = km_ref[t]
        r = r_ref[t]
        for vp in range(nv):
            s = s_ref[vp]
            sa = -jnp.sum(s * kk, axis=0, keepdims=True)
            s = s * w + sa * bm + v_ref[t, vp:vp + 1, :] * km
            s_ref[vp] = s
            o_ref[t, vp:vp + 1, :] = jnp.sum(s * r, axis=0, keepdims=True)
        return 0

    lax.fori_loop(0, tc, step, 0)


def rw_scan(kk, w, bm, km, r, v, s0):
    t_len, kdim, lanes = kk.shape
    nv = v.shape[1]
    tc = min(t_len, 16)
    op_spec = pl.BlockSpec((tc, kdim, lanes), lambda i: (i, 0, 0))
    v_spec = pl.BlockSpec((tc, nv, lanes), lambda i: (i, 0, 0))
    s_spec = pl.BlockSpec((nv, kdim, lanes), lambda i: (0, 0, 0))
    return pl.pallas_call(
        functools.partial(_rw_scan_body, nv=nv, tc=tc),
        grid=(t_len // tc,),
        in_specs=[op_spec] * 5 + [v_spec, s_spec],
        out_specs=[v_spec, s_spec],
        out_shape=[jax.ShapeDtypeStruct((t_len, nv, lanes), F32),
                   jax.ShapeDtypeStruct((nv, kdim, lanes), F32)],
        compiler_params=_params("arbitrary"),
        name="rw_scan",
    )(kk, w, bm, km, r, v, s0)


def _rw_post_body(o_ref, bonus_ref, g_ref, lw_ref, lb_ref, ones_ref, out_ref):
    o = o_ref[...]
    ones_bd = ones_ref[...]
    inv = 1.0 / RW_HEAD_DIM
    mu = _head_sum(o, ones_bd) * inv
    d = o - mu
    var = _head_sum(d * d, ones_bd) * inv
    y = d * lax.rsqrt(var + GN_EPS) * lw_ref[...] + lb_ref[...]
    out_ref[...] = ((y + bonus_ref[...]) * g_ref[...]).astype(out_ref.dtype)


def rw_post(o, bonus, g, lnx_w, lnx_b, ones_bd):
    rows = o.shape[0]
    tt = min(rows, 512)
    row_spec = pl.BlockSpec((tt, RW_WIDTH), lambda i: (i, 0))
    vec_spec = pl.BlockSpec((1, RW_WIDTH), lambda i: (0, 0))
    return pl.pallas_call(
        _rw_post_body,
        grid=(rows // tt,),
        in_specs=[row_spec, row_spec, row_spec, vec_spec, vec_spec,
                  pl.BlockSpec((RW_WIDTH, RW_WIDTH), lambda i: (0, 0))],
        out_specs=row_spec,
        out_shape=jax.ShapeDtypeStruct((rows, RW_WIDTH), BF16),
        compiler_params=_params("parallel"),
        name="rw_post",
    )(o, bonus, g, lnx_w, lnx_b, ones_bd)


def rwkv7(zb, z_prev, wkv0, p, batch, seq):
    r, w, km, kk, bm, v, g, bonus = rw_prep(zb, z_prev, p, batch, seq)
    bh = batch * RW_HEADS
    vpar = 128 // bh

    def key_layout(x):
        x = x.reshape(batch, seq, RW_HEADS, RW_HEAD_DIM).transpose(1, 3, 0, 2).reshape(seq, RW_HEAD_DIM, bh)
        return jnp.concatenate([x] * vpar, axis=-1)

    nv = RW_HEAD_DIM // vpar
    v_l = v.reshape(batch, seq, RW_HEADS, nv, vpar).transpose(1, 3, 4, 0, 2).reshape(seq, nv, 128)
    s0 = wkv0.reshape(batch, RW_HEADS, nv, vpar, RW_HEAD_DIM).transpose(2, 4, 3, 0, 1).reshape(
        nv, RW_HEAD_DIM, 128)
    o_l, s_l = rw_scan(key_layout(kk), key_layout(w), key_layout(bm), key_layout(km), key_layout(r), v_l, s0)
    o = o_l.reshape(seq, nv, vpar, batch, RW_HEADS).transpose(3, 0, 4, 1, 2).reshape(batch * seq, RW_WIDTH)
    s_fin = s_l.reshape(nv, RW_HEAD_DIM, vpar, batch, RW_HEADS).transpose(3, 4, 0, 2, 1).reshape(
        batch, RW_HEADS, RW_HEAD_DIM, RW_HEAD_DIM)
    out = rw_post(o, bonus, g, p["lnx_w"], p["lnx_b"], p["ones_bd"])
    return out, s_fin


def _gmlp_body(u_ref, v_ref, lnw_ref, lnb_ref, ws_ref, bs_ref, o_ref, *vn_refs, inner):
    v = v_ref[...]
    mu = jnp.mean(v, axis=-1, keepdims=True)
    var = jnp.mean(jnp.square(v - mu), axis=-1, keepdims=True)
    vn = (v - mu) * lax.rsqrt(var + LN_EPS) * lnw_ref[...] + lnb_ref[...]
    if vn_refs:
        vn_refs[0][...] = vn
    n = v.shape[0]
    row = lax.broadcasted_iota(jnp.int32, (n, n), 0)
    col = lax.broadcasted_iota(jnp.int32, (n, n), 1)
    causal = (col <= row) & (col >= row - row % inner)
    for gi in range(C_GROUPS):
        sl = slice(gi * C_GROUP_DIM, (gi + 1) * C_GROUP_DIM)
        wg = jnp.where(causal, ws_ref[gi], 0.0).astype(BF16)
        mixed = jnp.dot(wg, vn[:, sl].astype(BF16), preferred_element_type=F32) + bs_ref[:, gi:gi + 1]
        o_ref[:, sl] = (u_ref[:, sl] * mixed).astype(o_ref.dtype)


def gmlp_gate(z, ln_w, ln_b, ws, bs_t, rows_per_step, inner, want_v):
    rows = z.shape[0]
    n = rows_per_step
    out_shape = [jax.ShapeDtypeStruct((rows, C_WIDTH), BF16)]
    out_specs = [pl.BlockSpec((n, C_WIDTH), lambda i: (i, 0))]
    if want_v:
        out_shape.append(jax.ShapeDtypeStruct((rows, C_WIDTH), F32))
        out_specs.append(pl.BlockSpec((n, C_WIDTH), lambda i: (i, 0)))
    res = pl.pallas_call(
        functools.partial(_gmlp_body, inner=inner),
        grid=(rows // n,),
        in_specs=[pl.BlockSpec((n, C_WIDTH), lambda i: (i, 0)),
                  pl.BlockSpec((n, C_WIDTH), lambda i: (i, 1)),
                  pl.BlockSpec((1, C_WIDTH), lambda i: (0, 0)),
                  pl.BlockSpec((1, C_WIDTH), lambda i: (0, 0)),
                  pl.BlockSpec((C_GROUPS, n, n), lambda i: (0, 0, 0)),
                  pl.BlockSpec((n, C_GROUPS), lambda i: (0, 0))],
        out_specs=out_specs,
        out_shape=out_shape,
        compiler_params=_params("parallel"),
        name="gmlp_gate",
    )(z, z, ln_w, ln_b, ws, bs_t)
    return res if want_v else (res[0], None)


def _conv_gate_body(ug_ref, uv_ref, hg_ref, hv_ref, pg_ref, pv_ref, wg_ref, wv_ref, bg_ref, bv_ref, o_ref):
    i = pl.program_id(1)

    def conv(up_ref, halo_ref, prev_ref, w_ref, b_ref):
        up = up_ref[...]
        h2 = jnp.where(i == 0, prev_ref[...], halo_ref[6:8, :])
        row = lax.broadcasted_iota(jnp.int32, up.shape, 0)
        x1 = jnp.where(row == 0, h2[1:2, :], pltpu.roll(up, 1, axis=0))
        x2 = jnp.where(row == 0, h2[0:1, :], jnp.where(row == 1, h2[1:2, :], pltpu.roll(up, 2, axis=0)))
        return b_ref[...] + (x2 * w_ref[0:1, :] + x1 * w_ref[1:2, :] + up * w_ref[2:3, :])

    gate = conv(ug_ref, hg_ref, pg_ref, wg_ref, bg_ref)
    val = conv(uv_ref, hv_ref, pv_ref, wv_ref, bv_ref)
    o_ref[...] = (jax.nn.silu(gate) * val).astype(o_ref.dtype)


def conv_gate(up, prev, conv_w, conv_b, batch, seq):
    tt = min(seq, 256)
    nt = seq // tt
    hb = tt // 8
    tn = 1024
    nj = FFN_DIM // tn

    def halo_map(off):
        return lambda b, i, j: (jnp.maximum((b * nt + i) * hb - 1, 0), j + off)

    def col_spec(shape, off):
        return pl.BlockSpec(shape, lambda b, i, j: (0, j + off))

    return pl.pallas_call(
        _conv_gate_body,
        grid=(batch, nt, nj),
        in_specs=[pl.BlockSpec((tt, tn), lambda b, i, j: (b * nt + i, j)),
                  pl.BlockSpec((tt, tn), lambda b, i, j: (b * nt + i, j + nj)),
                  pl.BlockSpec((8, tn), halo_map(0)),
                  pl.BlockSpec((8, tn), halo_map(nj)),
                  pl.BlockSpec((None, 2, tn), lambda b, i, j: (b, 0, j)),
                  pl.BlockSpec((None, 2, tn), lambda b, i, j: (b, 0, j + nj)),
                  col_spec((CONV_W, tn), 0), col_spec((CONV_W, tn), nj),
                  col_spec((1, tn), 0), col_spec((1, tn), nj)],
        out_specs=pl.BlockSpec((tt, tn), lambda b, i, j: (b * nt + i, j)),
        out_shape=jax.ShapeDtypeStruct((batch * seq, FFN_DIM), BF16),
        compiler_params=_params("parallel", "parallel", "parallel"),
        name="conv_gate",
    )(up, up, up, up, prev, prev, conv_w, conv_w, conv_b, conv_b)


def _pad_cols(x, segs):
    c = 3 * RW_WIDTH
    parts = [x[..., :c]]
    start = c
    for width, padded in segs:
        seg = x[..., start:start + width]
        parts.append(jnp.pad(seg, [(0, 0)] * (x.ndim - 1) + [(0, padded - width)]))
        start += width
    return jnp.concatenate(parts, axis=-1)


_LORA_SEGS = ((LORA_W, 128), (LORA_A, 128), (LORA_G, 256))


def _unpad_cols(x):
    c = 3 * RW_WIDTH
    return jnp.concatenate([x[..., :c], x[..., c:c + LORA_W], x[..., c + 128:c + 128 + LORA_A],
                            x[..., c + 256:c + 256 + LORA_G]], axis=-1)


def _pad_rows(w, padded):
    return jnp.pad(w, ((0, padded - w.shape[0]), (0, 0)))


def kernel(x_prompt, x_sample, cache_k, cache_v, page_table, state_shift, state_wkv, state_conv, norm_mix, norm_ffn, norm_final, w_in_even, sb_bias, mu_shift, w0, w2, a0, a2, g2, k_k, k_a, r_k, lnx_w, lnx_b, w_out_even, w_in_odd, ln_v_w, ln_v_b, w_spatial, b_spatial, w_out_odd, w_up, conv_w, conv_b, w_down):
    bp, sp = x_prompt.shape[:2]
    bs, ts = x_sample.shape[:2]
    xp = x_prompt.reshape(bp * sp, D_MODEL)
    xs = x_sample.reshape(bs * ts, D_MODEL)
    groups = ((bp, sp), (bs, ts))

    head_id = jnp.arange(RW_WIDTH) // RW_HEAD_DIM
    ones_bd = (head_id[:, None] == head_id[None, :]).astype(BF16)

    k_out, v_out, sh_out, wkv_out, cv_out, chunkv_out = ([], []), ([], []), ([], []), ([], []), ([], []), []
    xs_all = [xp, xs]
    for l in range(DEPTH):
        i = l // 2
        h_all = [rmsnorm(x, norm_mix[l], BF16) for x in xs_all]
        if l % 2 == 0:
            w_qkv = w_in_even[i][:, :3 * SB_WIDTH].astype(BF16)
            w_rw = _pad_cols(w_in_even[i][:, 3 * SB_WIDTH:], _LORA_SEGS).astype(BF16)
            w_o = w_out_even[i].astype(BF16)
            w2h, w2l = _split_bf16(_pad_rows(w2[i], 128))
            a2h, a2l = _split_bf16(_pad_rows(a2[i], 128))
            g2h, g2l = _split_bf16(_pad_rows(g2[i], 256))
            p = dict(mu=_pad_cols(mu_shift[i][None, :], _LORA_SEGS), w0=w0[i][None, :], a0=a0[i][None, :],
                     k_k=k_k[i][None, :], k_a=k_a[i][None, :], r_k=r_k[i].reshape(1, RW_WIDTH),
                     w2h=w2h, w2l=w2l, a2h=a2h, a2l=a2l, g2h=g2h, g2l=g2l, ones_bd=ones_bd,
                     lnx_w=lnx_w[i][None, :], lnx_b=lnx_b[i][None, :])
            for gi, (nb, nt) in enumerate(groups):
                h = h_all[gi]
                qkv = matmul(h, w_qkv)
                zb = matmul(h, w_rw)
                k_rows = qkv[:, SB_WIDTH:2 * SB_WIDTH]
                v_rows = qkv[:, 2 * SB_WIDTH:]
                if gi == 0:
                    att = sb_prompt(qkv, sb_bias[i], nb, nt)
                    z_prev = jnp.zeros((nb, RW_PAD), F32)
                    wkv0 = jnp.zeros((nb, RW_HEADS, RW_HEAD_DIM, RW_HEAD_DIM), F32)
                else:
                    att = sb_sample(qkv[:, :SB_WIDTH], k_rows, v_rows, sb_bias[i], cache_k, cache_v, i,
                                    page_table)
                    z_prev = _pad_cols(state_shift[i], _LORA_SEGS)
                    wkv0 = state_wkv[i]
                rw, s_fin = rwkv7(zb, z_prev, wkv0, p, nb, nt)
                xs_all[gi] = matmul(jnp.concatenate([att, rw], axis=-1), w_o, res=xs_all[gi])
                k_out[gi].append(k_rows.reshape(nb, nt, SB_HEADS, SB_HEAD_DIM))
                v_out[gi].append(v_rows.reshape(nb, nt, SB_HEADS, SB_HEAD_DIM))
                sh_out[gi].append(_unpad_cols(zb.reshape(nb, nt, RW_PAD)[:, -1]))
                wkv_out[gi].append(s_fin)
        else:
            w_i = w_in_odd[i].astype(BF16)
            w_o = w_out_odd[i].astype(BF16)
            lnw = ln_v_w[i][None, :]
            lnb = ln_v_b[i][None, :]
            for gi, (nb, nt) in enumerate(groups):
                z = matmul(h_all[gi], w_i, act="gelu")
                if gi == 0:
                    gated, _ = gmlp_gate(z, lnw, lnb, w_spatial[i], b_spatial[i].T, CHUNK, CHUNK, False)
                else:
                    eye = jnp.eye(nb, dtype=F32)
                    ws_s = w_spatial[i][:, :nt, :nt]
                    ws_bd = (eye[None, :, None, :, None] * ws_s[:, None, :, None, :]).reshape(
                        C_GROUPS, nb * nt, nb * nt)
                    bs_t = jnp.tile(b_spatial[i][:, :nt].T, (nb, 1))
                    gated, v_rows = gmlp_gate(z, lnw, lnb, ws_bd, bs_t, nb * nt, nt, True)
                    chunkv_out.append(v_rows.reshape(nb, nt, C_WIDTH))
                xs_all[gi] = matmul(gated, w_o, res=xs_all[gi])
        w_u = w_up[l].astype(BF16)
        w_d = w_down[l].astype(BF16)
        for gi, (nb, nt) in enumerate(groups):
            hf = rmsnorm(xs_all[gi], norm_ffn[l], BF16)
            up = matmul(hf, w_u)
            prev = jnp.zeros((nb, CONV_W - 1, 2 * FFN_DIM), F32) if gi == 0 else state_conv[l]
            hid = conv_gate(up, prev, conv_w[l], conv_b[l][None, :], nb, nt)
            xs_all[gi] = matmul(hid, w_d, res=xs_all[gi])
            ext_tail = jnp.concatenate([prev, up.reshape(nb, nt, 2 * FFN_DIM)[:, -(CONV_W - 1):]], axis=1)
            cv_out[gi].append(ext_tail[:, -(CONV_W - 1):])
    y_prompt = rmsnorm(xs_all[0], norm_final, F32).reshape(bp, sp, D_MODEL)
    y_sample = rmsnorm(xs_all[1], norm_final, F32).reshape(bs, ts, D_MODEL)
    st = jnp.stack
    return (y_prompt, y_sample, st(k_out[0]), st(v_out[0]), st(k_out[1]), st(v_out[1]),
            st(sh_out[0]), st(sh_out[1]), st(wkv_out[0]), st(wkv_out[1]),
            st(cv_out[0]), st(cv_out[1]), st(chunkv_out))
```

```python
import functools

import jax
import jax.numpy as jnp
from jax import lax
from jax.experimental import pallas as pl
from jax.experimental.pallas import tpu as pltpu

F32 = jnp.float32
BF16 = jnp.bfloat16

D_MODEL = 2048
DEPTH = 4
PAGE_SIZE = 128
SB_HEAD_DIM = 128
SB_HEADS = 8
SB_WIDTH = 1024
RW_HEAD_DIM = 64
RW_WIDTH = 1024
RW_HEADS = 16
LORA_W = 64
LORA_A = 64
LORA_G = 160
RW_IN = 3 * RW_WIDTH + LORA_W + LORA_A + LORA_G
RW_PAD = 3 * RW_WIDTH + 128 + 128 + 256
GN_EPS = 64e-5
C_WIDTH = 2 * D_MODEL
C_GROUPS = 8
C_GROUP_DIM = C_WIDTH // C_GROUPS
CHUNK = 128
FFN_DIM = 2 * D_MODEL
CONV_W = 3
RMS_EPS = 1e-6
LN_EPS = 1e-5
SB_SCALE = SB_HEAD_DIM ** -0.5

VMEM_LIMIT_BYTES = 52 * 1024 * 1024


def _params(*sem):
    return pltpu.CompilerParams(dimension_semantics=sem, vmem_limit_bytes=VMEM_LIMIT_BYTES)


def _rmsnorm_body(x_ref, g_ref, o_ref):
    x = x_ref[...]
    ms = jnp.mean(x * x, axis=-1, keepdims=True)
    o_ref[...] = (x * lax.rsqrt(ms + RMS_EPS) * g_ref[...]).astype(o_ref.dtype)


def rmsnorm(x, g, out_dtype):
    m, d = x.shape
    tm = min(m, 512)
    return pl.pallas_call(
        _rmsnorm_body,
        grid=(m // tm,),
        in_specs=[pl.BlockSpec((tm, d), lambda i: (i, 0)),
                  pl.BlockSpec((1, d), lambda i: (0, 0))],
        out_specs=pl.BlockSpec((tm, d), lambda i: (i, 0)),
        out_shape=jax.ShapeDtypeStruct((m, d), out_dtype),
        compiler_params=_params("parallel"),
        name="rmsnorm",
    )(x, g.reshape(1, d))


def _matmul_body(*refs, act, has_res):
    if has_res:
        x_ref, w_ref, r_ref, o_ref = refs
    else:
        x_ref, w_ref, o_ref = refs
    acc = jnp.dot(x_ref[...], w_ref[...], preferred_element_type=F32)
    if act == "gelu":
        acc = jax.nn.gelu(acc)
    if has_res:
        acc = r_ref[...] + acc
    o_ref[...] = acc.astype(o_ref.dtype)


def matmul(x, w, res=None, act=None, out_dtype=F32):
    m, k = x.shape
    n = w.shape[1]
    tm = min(m, 1024)
    tn = 1024 if (k <= 2048 and n % 1024 == 0) else 512
    in_specs = [pl.BlockSpec((tm, k), lambda i, j: (i, 0)),
                pl.BlockSpec((k, tn), lambda i, j: (0, j))]
    args = [x, w]
    if res is not None:
        in_specs.append(pl.BlockSpec((tm, tn), lambda i, j: (i, j)))
        args.append(res)
    return pl.pallas_call(
        functools.partial(_matmul_body, act=act, has_res=res is not None),
        grid=(m // tm, n // tn),
        in_specs=in_specs,
        out_specs=pl.BlockSpec((tm, tn), lambda i, j: (i, j)),
        out_shape=jax.ShapeDtypeStruct((m, n), out_dtype),
        compiler_params=_params("parallel", "parallel"),
        name="matmul",
    )(*args)


def _tri_and_mask(n):
    row = lax.broadcasted_iota(jnp.int32, (n, n), 0)
    col = lax.broadcasted_iota(jnp.int32, (n, n), 1)
    tri = jnp.where(row > col, 1.0, 0.0).astype(BF16)
    return tri, row, col


SB_SUB = 128


def _sb_block(q, kb, vb, bias, carry, acc, tri, mask):
    z = lax.dot_general(q, kb, (((1,), (1,)), ((), ())), preferred_element_type=F32) + bias
    sp = jnp.log1p(jnp.exp(-jnp.abs(z)))
    log_beta = jnp.minimum(z, 0.0) - sp
    log_stay = -jnp.maximum(z, 0.0) - sp
    if mask is not None:
        log_stay = jnp.where(mask, log_stay, 0.0)
    hi = log_stay.astype(BF16)
    lo = (log_stay - hi.astype(F32)).astype(BF16)
    parts = [None] * (z.shape[1] // SB_SUB)
    for s in reversed(range(len(parts))):
        sl = slice(s * SB_SUB, (s + 1) * SB_SUB)
        parts[s] = (carry + jnp.dot(hi[:, sl], tri, preferred_element_type=F32)
                    + jnp.dot(lo[:, sl], tri, preferred_element_type=F32))
        carry = carry + jnp.sum(log_stay[:, sl], axis=1, keepdims=True)
    between = parts[0] if len(parts) == 1 else jnp.concatenate(parts, axis=1)
    att = jnp.exp(log_beta + between)
    if mask is not None:
        att = jnp.where(mask, att, 0.0)
    acc = acc + jnp.dot(att.astype(BF16), vb, preferred_element_type=F32)
    return carry, acc


def _sb_prompt_body(bias_ref, q_ref, k_ref, v_ref, o_ref, *, tq):
    h = pl.program_id(1)
    i = pl.program_id(2)
    bias = bias_ref[h]
    q = (q_ref[...] * SB_SCALE).astype(BF16)
    tri, _, _ = _tri_and_mask(SB_SUB)
    row = lax.broadcasted_iota(jnp.int32, (tq, tq), 0)
    col = lax.broadcasted_iota(jnp.int32, (tq, tq), 1)

    def blk(j, carry, acc, mask):
        off = pl.multiple_of(j * tq, tq)
        kb = k_ref[pl.ds(off, tq), :].astype(BF16)
        vb = v_ref[pl.ds(off, tq), :].astype(BF16)
        return _sb_block(q, kb, vb, bias, carry, acc, tri, mask)

    carry = jnp.zeros((tq, 1), F32)
    acc = jnp.zeros((tq, SB_HEAD_DIM), F32)
    carry, acc = blk(i, carry, acc, col < row)

    def body(n, c):
        return blk(i - 1 - n, c[0], c[1], None)

    carry, acc = lax.fori_loop(0, i, body, (carry, acc))
    o_ref[...] = acc.astype(o_ref.dtype)


def sb_prompt(qkv, bias, batch, seq):
    tq = 512
    nq = seq // tq
    return pl.pallas_call(
        functools.partial(_sb_prompt_body, tq=tq),
        grid=(batch, SB_HEADS, nq),
        in_specs=[pl.BlockSpec(memory_space=pltpu.SMEM),
                  pl.BlockSpec((tq, SB_HEAD_DIM), lambda b, h, i: (b * nq + i, h)),
                  pl.BlockSpec((seq, SB_HEAD_DIM), lambda b, h, i: (b, SB_HEADS + h)),
                  pl.BlockSpec((seq, SB_HEAD_DIM), lambda b, h, i: (b, 2 * SB_HEADS + h))],
        out_specs=pl.BlockSpec((tq, SB_HEAD_DIM), lambda b, h, i: (b * nq + i, h)),
        out_shape=jax.ShapeDtypeStruct((batch * seq, SB_WIDTH), BF16),
        compiler_params=_params("parallel", "parallel", "arbitrary"),
        name="sb_prompt",
    )(bias, qkv, qkv, qkv)


SB_PAGES_PER_STEP = 8


def _sb_sample_body(pt_ref, qbd_ref, bias_ref, kn_ref, vn_ref, *rest, t_new):
    npg = SB_PAGES_PER_STEP
    k_refs, v_refs = rest[:npg], rest[npg:2 * npg]
    o_ref, carry_ref, acc_ref = rest[2 * npg:]
    j = pl.program_id(1)
    rows = qbd_ref.shape[0]
    q = qbd_ref[...]
    bias = bias_ref[...]
    tri, _, _ = _tri_and_mask(SB_SUB)

    @pl.when(j == 0)
    def _():
        row = lax.broadcasted_iota(jnp.int32, (rows, PAGE_SIZE), 0)
        col = lax.broadcasted_iota(jnp.int32, (rows, PAGE_SIZE), 1)
        mask = col < (row % t_new)
        carry, acc = _sb_block(q, kn_ref[...].astype(BF16), vn_ref[...].astype(BF16), bias[:, :PAGE_SIZE],
                               jnp.zeros((rows, 1), F32), jnp.zeros(acc_ref.shape, F32), tri, mask)
        carry_ref[...] = carry
        acc_ref[...] = acc

    def pages(refs):
        return jnp.concatenate(
            [jnp.concatenate([r[pl.ds(h, PAGE_SIZE, stride=SB_HEADS), :].astype(BF16)
                              for h in range(SB_HEADS)], axis=1) for r in refs], axis=0)

    @pl.when(j > 0)
    def _():
        carry, acc = _sb_block(q, pages(k_refs), pages(v_refs), bias,
                               carry_ref[...], acc_ref[...], tri, None)
        carry_ref[...] = carry
        acc_ref[...] = acc

    @pl.when(j == pl.num_programs(1) - 1)
    def _():
        o_ref[...] = acc_ref[...]


def sb_sample(q, k_new, v_new, bias, cache_k, cache_v, layer, page_table):
    bd, n_pages = page_table.shape
    t_new = q.shape[0] // bd
    rows = SB_HEADS * t_new
    n_pool = cache_k.shape[1]
    npg = SB_PAGES_PER_STEP
    q4 = (q * SB_SCALE).reshape(bd, t_new, SB_HEADS, SB_HEAD_DIM).transpose(0, 2, 1, 3)
    eye = jnp.eye(SB_HEADS, dtype=F32)
    qbd = (q4[:, :, :, None, :] * eye[None, :, None, :, None]).reshape(bd, rows, SB_WIDTH).astype(BF16)
    bias_rows = jnp.broadcast_to(jnp.repeat(bias, t_new)[:, None], (rows, npg * PAGE_SIZE))
    pad = ((0, 0), (0, PAGE_SIZE - t_new), (0, 0))
    kn = jnp.pad(k_new.reshape(bd, t_new, SB_WIDTH), pad)
    vn = jnp.pad(v_new.reshape(bd, t_new, SB_WIDTH), pad)
    page_rows = PAGE_SIZE * SB_HEADS
    kc = cache_k.reshape(cache_k.shape[0], n_pool, page_rows, SB_HEAD_DIM)
    vc = cache_v.reshape(cache_v.shape[0], n_pool, page_rows, SB_HEAD_DIM)

    def page_spec(m):
        def page_map(b, j, pt):
            return (layer, pt[(b + 1) * n_pages - npg * jnp.maximum(j, 1) + m], 0, 0)
        return pl.BlockSpec((None, None, page_rows, SB_HEAD_DIM), page_map)

    page_specs = [page_spec(m) for m in range(npg)]
    out = pl.pallas_call(
        functools.partial(_sb_sample_body, t_new=t_new),
        grid_spec=pltpu.PrefetchScalarGridSpec(
            num_scalar_prefetch=1,
            grid=(bd, n_pages // npg + 1),
            in_specs=[pl.BlockSpec((None, rows, SB_WIDTH), lambda b, j, pt: (b, 0, 0)),
                      pl.BlockSpec((rows, npg * PAGE_SIZE), lambda b, j, pt: (0, 0)),
                      pl.BlockSpec((None, PAGE_SIZE, SB_WIDTH), lambda b, j, pt: (b, 0, 0)),
                      pl.BlockSpec((None, PAGE_SIZE, SB_WIDTH), lambda b, j, pt: (b, 0, 0))]
            + page_specs + page_specs,
            out_specs=pl.BlockSpec((None, rows, SB_WIDTH), lambda b, j, pt: (b, 0, 0)),
            scratch_shapes=[pltpu.VMEM((rows, 1), F32), pltpu.VMEM((rows, SB_WIDTH), F32)]),
        out_shape=jax.ShapeDtypeStruct((bd, rows, SB_WIDTH), F32),
        compiler_params=_params("parallel", "arbitrary"),
        name="sb_sample",
    )(page_table.reshape(-1), qbd, bias_rows, kn, vn, *([kc] * npg), *([vc] * npg))
    o5 = out.reshape(bd, SB_HEADS, t_new, SB_HEADS, SB_HEAD_DIM)
    idx = jnp.arange(SB_HEADS)
    diag = o5[:, idx, :, idx, :]
    return diag.transpose(1, 2, 0, 3).reshape(bd * t_new, SB_WIDTH).astype(BF16)


def _split_bf16(x):
    hi = x.astype(BF16)
    lo = (x - hi.astype(F32)).astype(BF16)
    return hi, lo


def _dot3(a, b_hi, b_lo):
    a_hi, a_lo = _split_bf16(a)
    return (jnp.dot(a_hi, b_hi, preferred_element_type=F32)
            + jnp.dot(a_hi, b_lo, preferred_element_type=F32)
            + jnp.dot(a_lo, b_hi, preferred_element_type=F32))


def _head_sum(x, ones_bd):
    hi, lo = _split_bf16(x)
    return (jnp.dot(hi, ones_bd, preferred_element_type=F32)
            + jnp.dot(lo, ones_bd, preferred_element_type=F32))


def _rw_prep_body(zb_ref, halo_ref, prev_ref, mu_ref, w0_ref, a0_ref, kk_ref_w, ka_ref, rk_ref,
                  w2h_ref, w2l_ref, a2h_ref, a2l_ref, g2h_ref, g2l_ref, ones_ref,
                  r_out, w_out, km_out, kk_out, b_out, v_out, g_out, bonus_out):
    i = pl.program_id(1)
    zb = zb_ref[...]
    prev_row = jnp.where(i == 0, prev_ref[...], halo_ref[7:8, :])
    row = lax.broadcasted_iota(jnp.int32, zb.shape, 0)
    shifted = jnp.where(row == 0, prev_row, pltpu.roll(zb, 1, axis=0))
    xz = zb + (shifted - zb) * mu_ref[...]
    c = RW_WIDTH
    r = xz[:, 0:c]
    k = xz[:, c:2 * c]
    v = xz[:, 2 * c:3 * c]
    zw = xz[:, 3 * c:3 * c + 128]
    za = xz[:, 3 * c + 128:3 * c + 256]
    zg = xz[:, 3 * c + 256:3 * c + 512]
    ones_bd = ones_ref[...]
    w_log = -jax.nn.softplus(-(w0_ref[...] + _dot3(jnp.tanh(zw), w2h_ref[...], w2l_ref[...]))) - 0.5
    decay = jnp.exp(-jnp.exp(w_log))
    a = jax.nn.sigmoid(a0_ref[...] + _dot3(za, a2h_ref[...], a2l_ref[...]))
    g = _dot3(jax.nn.sigmoid(zg), g2h_ref[...], g2l_ref[...])
    kk = k * kk_ref_w[...]
    kk = kk / jnp.maximum(jnp.sqrt(_head_sum(kk * kk, ones_bd)), 1e-12)
    k_mod = k * (1.0 + (a - 1.0) * ka_ref[...])
    bonus = _head_sum(r * k_mod * rk_ref[...], ones_bd) * v
    r_out[...] = r
    w_out[...] = decay
    km_out[...] = k_mod
    kk_out[...] = kk
    b_out[...] = kk * a
    v_out[...] = v
    g_out[...] = g
    bonus_out[...] = bonus


def rw_prep(zb, z_prev, p, batch, seq):
    tt = min(seq, 256)
    nt = seq // tt
    rows = batch * seq
    hb = tt // 8

    def row_spec(width):
        return pl.BlockSpec((tt, width), lambda b, i: (b * nt + i, 0))

    def const_spec(shape):
        return pl.BlockSpec(shape, lambda b, i: (0,) * len(shape))

    out_sds = jax.ShapeDtypeStruct((rows, RW_WIDTH), F32)
    return pl.pallas_call(
        _rw_prep_body,
        grid=(batch, nt),
        in_specs=[row_spec(RW_PAD),
                  pl.BlockSpec((8, RW_PAD), lambda b, i: (jnp.maximum((b * nt + i) * hb - 1, 0), 0)),
                  pl.BlockSpec((None, 1, RW_PAD), lambda b, i: (b, 0, 0)),
                  const_spec((1, RW_PAD)),
                  const_spec((1, RW_WIDTH)), const_spec((1, RW_WIDTH)), const_spec((1, RW_WIDTH)),
                  const_spec((1, RW_WIDTH)), const_spec((1, RW_WIDTH)),
                  const_spec((128, RW_WIDTH)), const_spec((128, RW_WIDTH)),
                  const_spec((128, RW_WIDTH)), const_spec((128, RW_WIDTH)),
                  const_spec((256, RW_WIDTH)), const_spec((256, RW_WIDTH)),
                  const_spec((RW_WIDTH, RW_WIDTH))],
        out_specs=[row_spec(RW_WIDTH)] * 8,
        out_shape=[out_sds] * 8,
        compiler_params=_params("parallel", "arbitrary"),
        name="rw_prep",
    )(zb, zb, z_prev.reshape(batch, 1, RW_PAD), p["mu"], p["w0"], p["a0"], p["k_k"], p["k_a"], p["r_k"],
      p["w2h"], p["w2l"], p["a2h"], p["a2l"], p["g2h"], p["g2l"], p["ones_bd"])


def _rw_scan_body(kk_ref, w_ref, b_ref, km_ref, r_ref, v_ref, s0_ref, o_ref, s_ref, *wide_refs, nv, tc):
    @pl.when(pl.program_id(0) == 0)
    def _():
        s_ref[...] = s0_ref[...]

    op_refs = (kk_ref, w_ref, b_ref, km_ref, r_ref)
    if wide_refs:
        (wide_ref,) = wide_refs
        reps = s_ref.shape[-1] // kk_ref.shape[-1]
        for n, ref in enumerate(op_refs):
            wide_ref[n] = jnp.concatenate([ref[...]] * reps, axis=-1)
        op_refs = tuple(wide_ref.at[n] for n in range(len(op_refs)))

    def step(t, _):
        kk, w, bm, km, r = (ref[t] for ref in op_refs)
        for vp in range(nv):
            s = s_ref[vp]
            sa = -jnp.sum(s * kk, axis=0, keepdims=True)
            s = s * w + sa * bm + v_ref[t, vp:vp + 1, :] * km
            s_ref[vp] = s
            o_ref[t, vp:vp + 1, :] = jnp.sum(s * r, axis=0, keepdims=True)
        return 0

    lax.fori_loop(0, tc, step, 0)


def rw_scan(kk, w, bm, km, r, v, s0):
    t_len, kdim, problems = kk.shape
    nv, lanes = v.shape[1:]
    tc = min(t_len, 16)
    op_spec = pl.BlockSpec((tc, kdim, problems), lambda i: (i, 0, 0))
    v_spec = pl.BlockSpec((tc, nv, lanes), lambda i: (i, 0, 0))
    s_spec = pl.BlockSpec((nv, kdim, lanes), lambda i: (0, 0, 0))
    scratch = [] if problems == lanes else [pltpu.VMEM((5, tc, kdim, lanes), F32)]
    return pl.pallas_call(
        functools.partial(_rw_scan_body, nv=nv, tc=tc),
        grid=(t_len // tc,),
        in_specs=[op_spec] * 5 + [v_spec, s_spec],
        out_specs=[v_spec, s_spec],
        out_shape=[jax.ShapeDtypeStruct((t_len, nv, lanes), F32),
                   jax.ShapeDtypeStruct((nv, kdim, lanes), F32)],
        scratch_shapes=scratch,
        compiler_params=_params("arbitrary"),
        name="rw_scan",
    )(kk, w, bm, km, r, v, s0)


def _rw_post_body(o_ref, bonus_ref, g_ref, lw_ref, lb_ref, ones_ref, out_ref):
    o = o_ref[...]
    ones_bd = ones_ref[...]
    inv = 1.0 / RW_HEAD_DIM
    mu = _head_sum(o, ones_bd) * inv
    d = o - mu
    var = _head_sum(d * d, ones_bd) * inv
    y = d * lax.rsqrt(var + GN_EPS) * lw_ref[...] + lb_ref[...]
    out_ref[...] = ((y + bonus_ref[...]) * g_ref[...]).astype(out_ref.dtype)


def rw_post(o, bonus, g, lnx_w, lnx_b, ones_bd):
    rows = o.shape[0]
    tt = min(rows, 512)
    row_spec = pl.BlockSpec((tt, RW_WIDTH), lambda i: (i, 0))
    vec_spec = pl.BlockSpec((1, RW_WIDTH), lambda i: (0, 0))
    return pl.pallas_call(
        _rw_post_body,
        grid=(rows // tt,),
        in_specs=[row_spec, row_spec, row_spec, vec_spec, vec_spec,
                  pl.BlockSpec((RW_WIDTH, RW_WIDTH), lambda i: (0, 0))],
        out_specs=row_spec,
        out_shape=jax.ShapeDtypeStruct((rows, RW_WIDTH), BF16),
        compiler_params=_params("parallel"),
        name="rw_post",
    )(o, bonus, g, lnx_w, lnx_b, ones_bd)


def rwkv7(zb, z_prev, wkv0, p, batch, seq):
    r, w, km, kk, bm, v, g, bonus = rw_prep(zb, z_prev, p, batch, seq)
    bh = batch * RW_HEADS
    vpar = 128 // bh

    def key_layout(x):
        return x.reshape(batch, seq, RW_HEADS, RW_HEAD_DIM).transpose(1, 3, 0, 2).reshape(seq, RW_HEAD_DIM, bh)

    nv = RW_HEAD_DIM // vpar
    v_l = v.reshape(batch, seq, RW_HEADS, nv, vpar).transpose(1, 3, 4, 0, 2).reshape(seq, nv, 128)
    s0 = wkv0.reshape(batch, RW_HEADS, nv, vpar, RW_HEAD_DIM).transpose(2, 4, 3, 0, 1).reshape(
        nv, RW_HEAD_DIM, 128)
    o_l, s_l = rw_scan(key_layout(kk), key_layout(w), key_layout(bm), key_layout(km), key_layout(r), v_l, s0)
    o = o_l.reshape(seq, nv, vpar, batch, RW_HEADS).transpose(3, 0, 4, 1, 2).reshape(batch * seq, RW_WIDTH)
    s_fin = s_l.reshape(nv, RW_HEAD_DIM, vpar, batch, RW_HEADS).transpose(3, 4, 0, 2, 1).reshape(
        batch, RW_HEADS, RW_HEAD_DIM, RW_HEAD_DIM)
    out = rw_post(o, bonus, g, p["lnx_w"], p["lnx_b"], p["ones_bd"])
    return out, s_fin


def _gmlp_body(u_ref, v_ref, lnw_ref, lnb_ref, ws_ref, bs_ref, o_ref, *vn_refs, inner):
    v = v_ref[...]
    mu = jnp.mean(v, axis=-1, keepdims=True)
    var = jnp.mean(jnp.square(v - mu), axis=-1, keepdims=True)
    vn = (v - mu) * lax.rsqrt(var + LN_EPS) * lnw_ref[...] + lnb_ref[...]
    if vn_refs:
        vn_refs[0][...] = vn
    n = v.shape[0]
    row = lax.broadcasted_iota(jnp.int32, (n, n), 0)
    col = lax.broadcasted_iota(jnp.int32, (n, n), 1)
    causal = (col <= row) & (col >= row - row % inner)
    for gi in range(C_GROUPS):
        sl = slice(gi * C_GROUP_DIM, (gi + 1) * C_GROUP_DIM)
        wg = jnp.where(causal, ws_ref[gi], 0.0).astype(BF16)
        mixed = jnp.dot(wg, vn[:, sl].astype(BF16), preferred_element_type=F32) + bs_ref[:, gi:gi + 1]
        o_ref[:, sl] = (u_ref[:, sl] * mixed).astype(o_ref.dtype)


def gmlp_gate(z, ln_w, ln_b, ws, bs_t, rows_per_step, inner, want_v):
    rows = z.shape[0]
    n = rows_per_step
    out_shape = [jax.ShapeDtypeStruct((rows, C_WIDTH), BF16)]
    out_specs = [pl.BlockSpec((n, C_WIDTH), lambda i: (i, 0))]
    if want_v:
        out_shape.append(jax.ShapeDtypeStruct((rows, C_WIDTH), F32))
        out_specs.append(pl.BlockSpec((n, C_WIDTH), lambda i: (i, 0)))
    res = pl.pallas_call(
        functools.partial(_gmlp_body, inner=inner),
        grid=(rows // n,),
        in_specs=[pl.BlockSpec((n, C_WIDTH), lambda i: (i, 0)),
                  pl.BlockSpec((n, C_WIDTH), lambda i: (i, 1)),
                  pl.BlockSpec((1, C_WIDTH), lambda i: (0, 0)),
                  pl.BlockSpec((1, C_WIDTH), lambda i: (0, 0)),
                  pl.BlockSpec((C_GROUPS, n, n), lambda i: (0, 0, 0)),
                  pl.BlockSpec((n, C_GROUPS), lambda i: (0, 0))],
        out_specs=out_specs,
        out_shape=out_shape,
        compiler_params=_params("parallel"),
        name="gmlp_gate",
    )(z, z, ln_w, ln_b, ws, bs_t)
    return res if want_v else (res[0], None)


def _conv_gate_body(ug_ref, uv_ref, hg_ref, hv_ref, pg_ref, pv_ref, wg_ref, wv_ref, bg_ref, bv_ref, o_ref):
    i = pl.program_id(1)

    def conv(up_ref, halo_ref, prev_ref, w_ref, b_ref):
        h2 = jnp.where(i == 0, prev_ref[...], halo_ref[6:8, :])
        return _conv3(up_ref[...], h2, w_ref, b_ref)

    gate = conv(ug_ref, hg_ref, pg_ref, wg_ref, bg_ref)
    val = conv(uv_ref, hv_ref, pv_ref, wv_ref, bv_ref)
    o_ref[...] = (jax.nn.silu(gate) * val).astype(o_ref.dtype)


def conv_gate(up, prev, conv_w, conv_b, batch, seq):
    tt = min(seq, 256)
    nt = seq // tt
    hb = tt // 8
    tn = 1024
    nj = FFN_DIM // tn

    def halo_map(off):
        return lambda b, i, j: (jnp.maximum((b * nt + i) * hb - 1, 0), j + off)

    def col_spec(shape, off):
        return pl.BlockSpec(shape, lambda b, i, j: (0, j + off))

    return pl.pallas_call(
        _conv_gate_body,
        grid=(batch, nt, nj),
        in_specs=[pl.BlockSpec((tt, tn), lambda b, i, j: (b * nt + i, j)),
                  pl.BlockSpec((tt, tn), lambda b, i, j: (b * nt + i, j + nj)),
                  pl.BlockSpec((8, tn), halo_map(0)),
                  pl.BlockSpec((8, tn), halo_map(nj)),
                  pl.BlockSpec((None, 2, tn), lambda b, i, j: (b, 0, j)),
                  pl.BlockSpec((None, 2, tn), lambda b, i, j: (b, 0, j + nj)),
                  col_spec((CONV_W, tn), 0), col_spec((CONV_W, tn), nj),
                  col_spec((1, tn), 0), col_spec((1, tn), nj)],
        out_specs=pl.BlockSpec((tt, tn), lambda b, i, j: (b * nt + i, j)),
        out_shape=jax.ShapeDtypeStruct((batch * seq, FFN_DIM), BF16),
        compiler_params=_params("parallel", "parallel", "parallel"),
        name="conv_gate",
    )(up, up, up, up, prev, prev, conv_w, conv_w, conv_b, conv_b)


def _conv3(up, h2, w_ref, b_ref):
    row = lax.broadcasted_iota(jnp.int32, up.shape, 0)
    x1 = jnp.where(row == 0, h2[1:2, :], pltpu.roll(up, 1, axis=0))
    x2 = jnp.where(row == 0, h2[0:1, :], jnp.where(row == 1, h2[1:2, :], pltpu.roll(up, 2, axis=0)))
    return b_ref[...] + (x2 * w_ref[0:1, :] + x1 * w_ref[1:2, :] + up * w_ref[2:3, :])


def _up_conv_body(x_ref, xh_ref, wg_ref, wv_ref, pg_ref, pv_ref, cwg_ref, cwv_ref, cbg_ref, cbv_ref,
                  o_ref, tg_ref, tv_ref, *, tiles_per_batch):
    first = pl.program_id(0) % tiles_per_batch == 0
    x = x_ref[...]
    xh = xh_ref[...]
    tm = x.shape[0]

    def half(w_ref, prev_ref, cw_ref, cb_ref, tail_ref):
        w = w_ref[...]
        up = jnp.dot(x, w, preferred_element_type=F32)
        up_before = jnp.dot(xh, w, preferred_element_type=F32)
        h2 = jnp.where(first, prev_ref[...], up_before[14:16, :])
        tail_ref[...] = up[tm - 8:, :]
        return _conv3(up, h2, cw_ref, cb_ref)

    gate = half(wg_ref, pg_ref, cwg_ref, cbg_ref, tg_ref)
    val = half(wv_ref, pv_ref, cwv_ref, cbv_ref, tv_ref)
    o_ref[...] = (jax.nn.silu(gate) * val).astype(o_ref.dtype)


def ffn_up_conv(x, w_up, prev, conv_w, conv_b, batch, seq):
    assert seq % 1024 == 0
    m, k = x.shape
    tm, tn = 1024, 512
    tiles_per_batch = seq // tm
    nj = FFN_DIM // tn
    hb = tm // 16

    def col_spec(shape, off):
        return pl.BlockSpec(shape, lambda i, j: (0, j + off))

    def batch_spec(rows, off):
        return pl.BlockSpec((None, rows, tn), lambda i, j: (i // tiles_per_batch, 0, j + off))

    hid, tail_g, tail_v = pl.pallas_call(
        functools.partial(_up_conv_body, tiles_per_batch=tiles_per_batch),
        grid=(m // tm, nj),
        in_specs=[pl.BlockSpec((tm, k), lambda i, j: (i, 0)),
                  pl.BlockSpec((16, k), lambda i, j: (jnp.maximum(i * hb - 1, 0), 0)),
                  col_spec((k, tn), 0), col_spec((k, tn), nj),
                  batch_spec(2, 0), batch_spec(2, nj),
                  col_spec((CONV_W, tn), 0), col_spec((CONV_W, tn), nj),
                  col_spec((1, tn), 0), col_spec((1, tn), nj)],
        out_specs=[pl.BlockSpec((tm, tn), lambda i, j: (i, j)),
                   pl.BlockSpec((None, 8, tn), lambda i, j: (i, 0, j)),
                   pl.BlockSpec((None, 8, tn), lambda i, j: (i, 0, j))],
        out_shape=[jax.ShapeDtypeStruct((m, FFN_DIM), BF16),
                   jax.ShapeDtypeStruct((m // tm, 8, FFN_DIM), F32),
                   jax.ShapeDtypeStruct((m // tm, 8, FFN_DIM), F32)],
        compiler_params=_params("parallel", "parallel"),
        name="ffn_up_conv",
    )(x, x, w_up, w_up, prev, prev, conv_w, conv_w, conv_b, conv_b)
    tails = jnp.concatenate([tail_g, tail_v], axis=-1)
    return hid, tails.reshape(batch, tiles_per_batch, 8, 2 * FFN_DIM)[:, -1]


def _pad_cols(x, segs):
    c = 3 * RW_WIDTH
    parts = [x[..., :c]]
    start = c
    for width, padded in segs:
        seg = x[..., start:start + width]
        parts.append(jnp.pad(seg, [(0, 0)] * (x.ndim - 1) + [(0, padded - width)]))
        start += width
    return jnp.concatenate(parts, axis=-1)


_LORA_SEGS = ((LORA_W, 128), (LORA_A, 128), (LORA_G, 256))


def _unpad_cols(x):
    c = 3 * RW_WIDTH
    return jnp.concatenate([x[..., :c], x[..., c:c + LORA_W], x[..., c + 128:c + 128 + LORA_A],
                            x[..., c + 256:c + 256 + LORA_G]], axis=-1)


def _pad_rows(w, padded):
    return jnp.pad(w, ((0, padded - w.shape[0]), (0, 0)))


def kernel(x_prompt, x_sample, cache_k, cache_v, page_table, state_shift, state_wkv, state_conv, norm_mix, norm_ffn, norm_final, w_in_even, sb_bias, mu_shift, w0, w2, a0, a2, g2, k_k, k_a, r_k, lnx_w, lnx_b, w_out_even, w_in_odd, ln_v_w, ln_v_b, w_spatial, b_spatial, w_out_odd, w_up, conv_w, conv_b, w_down):
    bp, sp = x_prompt.shape[:2]
    bs, ts = x_sample.shape[:2]
    xp = x_prompt.reshape(bp * sp, D_MODEL)
    xs = x_sample.reshape(bs * ts, D_MODEL)
    groups = ((bp, sp), (bs, ts))

    head_id = jnp.arange(RW_WIDTH) // RW_HEAD_DIM
    ones_bd = (head_id[:, None] == head_id[None, :]).astype(BF16)

    k_out, v_out, sh_out, wkv_out, cv_out, chunkv_out = ([], []), ([], []), ([], []), ([], []), ([], []), []
    xs_all = [xp, xs]
    for l in range(DEPTH):
        i = l // 2
        h_all = [rmsnorm(x, norm_mix[l], BF16) for x in xs_all]
        if l % 2 == 0:
            w_qkv = w_in_even[i][:, :3 * SB_WIDTH].astype(BF16)
            w_rw = _pad_cols(w_in_even[i][:, 3 * SB_WIDTH:], _LORA_SEGS).astype(BF16)
            w_o = w_out_even[i].astype(BF16)
            w2h, w2l = _split_bf16(_pad_rows(w2[i], 128))
            a2h, a2l = _split_bf16(_pad_rows(a2[i], 128))
            g2h, g2l = _split_bf16(_pad_rows(g2[i], 256))
            p = dict(mu=_pad_cols(mu_shift[i][None, :], _LORA_SEGS), w0=w0[i][None, :], a0=a0[i][None, :],
                     k_k=k_k[i][None, :], k_a=k_a[i][None, :], r_k=r_k[i].reshape(1, RW_WIDTH),
                     w2h=w2h, w2l=w2l, a2h=a2h, a2l=a2l, g2h=g2h, g2l=g2l, ones_bd=ones_bd,
                     lnx_w=lnx_w[i][None, :], lnx_b=lnx_b[i][None, :])
            for gi, (nb, nt) in enumerate(groups):
                h = h_all[gi]
                qkv = matmul(h, w_qkv)
                zb = matmul(h, w_rw)
                k_rows = qkv[:, SB_WIDTH:2 * SB_WIDTH]
                v_rows = qkv[:, 2 * SB_WIDTH:]
                if gi == 0:
                    att = sb_prompt(qkv, sb_bias[i], nb, nt)
                    z_prev = jnp.zeros((nb, RW_PAD), F32)
                    wkv0 = jnp.zeros((nb, RW_HEADS, RW_HEAD_DIM, RW_HEAD_DIM), F32)
                else:
                    att = sb_sample(qkv[:, :SB_WIDTH], k_rows, v_rows, sb_bias[i], cache_k, cache_v, i,
                                    page_table)
                    z_prev = _pad_cols(state_shift[i], _LORA_SEGS)
                    wkv0 = state_wkv[i]
                rw, s_fin = rwkv7(zb, z_prev, wkv0, p, nb, nt)
                xs_all[gi] = matmul(jnp.concatenate([att, rw], axis=-1), w_o, res=xs_all[gi])
                k_out[gi].append(k_rows.reshape(nb, nt, SB_HEADS, SB_HEAD_DIM))
                v_out[gi].append(v_rows.reshape(nb, nt, SB_HEADS, SB_HEAD_DIM))
                sh_out[gi].append(_unpad_cols(zb.reshape(nb, nt, RW_PAD)[:, -1]))
                wkv_out[gi].append(s_fin)
        else:
            w_i = w_in_odd[i].astype(BF16)
            w_o = w_out_odd[i].astype(BF16)
            lnw = ln_v_w[i][None, :]
            lnb = ln_v_b[i][None, :]
            for gi, (nb, nt) in enumerate(groups):
                z = matmul(h_all[gi], w_i, act="gelu")
                if gi == 0:
                    gated, _ = gmlp_gate(z, lnw, lnb, w_spatial[i], b_spatial[i].T, CHUNK, CHUNK, False)
                else:
                    eye = jnp.eye(nb, dtype=F32)
                    ws_s = w_spatial[i][:, :nt, :nt]
                    ws_bd = (eye[None, :, None, :, None] * ws_s[:, None, :, None, :]).reshape(
                        C_GROUPS, nb * nt, nb * nt)
                    bs_t = jnp.tile(b_spatial[i][:, :nt].T, (nb, 1))
                    gated, v_rows = gmlp_gate(z, lnw, lnb, ws_bd, bs_t, nb * nt, nt, True)
                    chunkv_out.append(v_rows.reshape(nb, nt, C_WIDTH))
                xs_all[gi] = matmul(gated, w_o, res=xs_all[gi])
        w_u = w_up[l].astype(BF16)
        w_d = w_down[l].astype(BF16)
        for gi, (nb, nt) in enumerate(groups):
            hf = rmsnorm(xs_all[gi], norm_ffn[l], BF16)
            if gi == 0:
                prev = jnp.zeros((nb, CONV_W - 1, 2 * FFN_DIM), F32)
                hid, up_tail = ffn_up_conv(hf, w_u, prev, conv_w[l], conv_b[l][None, :], nb, nt)
            else:
                prev = state_conv[l]
                up = matmul(hf, w_u)
                hid = conv_gate(up, prev, conv_w[l], conv_b[l][None, :], nb, nt)
                up_tail = up.reshape(nb, nt, 2 * FFN_DIM)
            xs_all[gi] = matmul(hid, w_d, res=xs_all[gi])
            cv_out[gi].append(jnp.concatenate([prev, up_tail], axis=1)[:, -(CONV_W - 1):])
    y_prompt = rmsnorm(xs_all[0], norm_final, F32).reshape(bp, sp, D_MODEL)
    y_sample = rmsnorm(xs_all[1], norm_final, F32).reshape(bs, ts, D_MODEL)
    st = jnp.stack
    return (y_prompt, y_sample, st(k_out[0]), st(v_out[0]), st(k_out[1]), st(v_out[1]),
            st(sh_out[0]), st(sh_out[1]), st(wkv_out[0]), st(wkv_out[1]),
            st(cv_out[0]), st(cv_out[1]), st(chunkv_out))
```

```python
import functools

import jax
import jax.numpy as jnp
from jax import lax
from jax.experimental import pallas as pl
from jax.experimental.pallas import tpu as pltpu

F32 = jnp.float32
BF16 = jnp.bfloat16

D_MODEL = 2048
DEPTH = 4
PAGE_SIZE = 128
SB_HEAD_DIM = 128
SB_HEADS = 8
SB_WIDTH = 1024
RW_HEAD_DIM = 64
RW_WIDTH = 1024
RW_HEADS = 16
LORA_W = 64
LORA_A = 64
LORA_G = 160
RW_IN = 3 * RW_WIDTH + LORA_W + LORA_A + LORA_G
RW_PAD = 3 * RW_WIDTH + 128 + 128 + 256
GN_EPS = 64e-5
C_WIDTH = 2 * D_MODEL
C_GROUPS = 8
C_GROUP_DIM = C_WIDTH // C_GROUPS
CHUNK = 128
FFN_DIM = 2 * D_MODEL
CONV_W = 3
RMS_EPS = 1e-6
LN_EPS = 1e-5
SB_SCALE = SB_HEAD_DIM ** -0.5

VMEM_LIMIT_BYTES = 52 * 1024 * 1024


def _params(*sem):
    return pltpu.CompilerParams(dimension_semantics=sem, vmem_limit_bytes=VMEM_LIMIT_BYTES)


def _rmsnorm_body(x_ref, g_ref, o_ref):
    x = x_ref[...]
    ms = jnp.mean(x * x, axis=-1, keepdims=True)
    o_ref[...] = (x * lax.rsqrt(ms + RMS_EPS) * g_ref[...]).astype(o_ref.dtype)


def rmsnorm(x, g, out_dtype):
    m, d = x.shape
    tm = min(m, 512)
    return pl.pallas_call(
        _rmsnorm_body,
        grid=(m // tm,),
        in_specs=[pl.BlockSpec((tm, d), lambda i: (i, 0)),
                  pl.BlockSpec((1, d), lambda i: (0, 0))],
        out_specs=pl.BlockSpec((tm, d), lambda i: (i, 0)),
        out_shape=jax.ShapeDtypeStruct((m, d), out_dtype),
        compiler_params=_params("parallel"),
        name="rmsnorm",
    )(x, g.reshape(1, d))


def _matmul_body(*refs, act, has_res, has_rider):
    refs = list(refs)
    wb_ref = refs.pop()
    n_in = (2 if has_rider else 1) * (2 if has_res else 1) + 1
    ins, outs = refs[:n_in], refs[n_in:]
    w_ref = ins[1 + has_rider]

    def product(x_ref, r_ref, o_ref):
        acc = jnp.dot(x_ref[...], wb_ref[...], preferred_element_type=F32)
        if act == "gelu":
            acc = jax.nn.gelu(acc)
        if has_res:
            acc = r_ref[...] + acc
        o_ref[...] = acc.astype(o_ref.dtype)

    @pl.when(pl.program_id(1) == 0)
    def _():
        wb_ref[...] = w_ref[...].astype(BF16)
        if has_rider:
            product(ins[1], ins[-1] if has_res else None, outs[1])

    product(ins[0], ins[2 + has_rider] if has_res else None, outs[0])


def matmul(x, w, col0=0, n=None, res=None, act=None, rider=None, rider_res=None):
    m, k = x.shape
    n = w.shape[1] - col0 if n is None else n
    tm = min(m, 1024)
    tn = 1024 if (k <= 2048 and n % 1024 == 0 and col0 % 1024 == 0) else 512
    c0 = col0 // tn
    has_res, has_rider = res is not None, rider is not None
    in_specs = [pl.BlockSpec((tm, k), lambda j, i: (i, 0))]
    args = [x]
    if has_rider:
        mr = rider.shape[0]
        in_specs.append(pl.BlockSpec((mr, k), lambda j, i: (0, 0)))
        args.append(rider)
    in_specs.append(pl.BlockSpec((k, tn), lambda j, i: (0, j + c0)))
    args.append(w)
    out_specs = [pl.BlockSpec((tm, tn), lambda j, i: (i, j))]
    out_shape = [jax.ShapeDtypeStruct((m, n), F32)]
    if has_res:
        in_specs.append(pl.BlockSpec((tm, tn), lambda j, i: (i, j)))
        args.append(res)
    if has_rider:
        out_specs.append(pl.BlockSpec((mr, tn), lambda j, i: (0, j)))
        out_shape.append(jax.ShapeDtypeStruct((mr, n), F32))
        if has_res:
            in_specs.append(pl.BlockSpec((mr, tn), lambda j, i: (0, j)))
            args.append(rider_res)
    out = pl.pallas_call(
        functools.partial(_matmul_body, act=act, has_res=has_res, has_rider=has_rider),
        grid=(n // tn, m // tm),
        in_specs=in_specs,
        out_specs=out_specs,
        out_shape=out_shape,
        scratch_shapes=[pltpu.VMEM((k, tn), BF16)],
        compiler_params=_params("parallel", "arbitrary"),
        name="matmul",
    )(*args)
    return tuple(out) if has_rider else out[0]


def _tri_and_mask(n):
    row = lax.broadcasted_iota(jnp.int32, (n, n), 0)
    col = lax.broadcasted_iota(jnp.int32, (n, n), 1)
    tri = jnp.where(row > col, 1.0, 0.0).astype(BF16)
    return tri, row, col


SB_SUB = 128


def _sb_block(q, kb, vb, bias, carry, acc, tri, mask):
    z = lax.dot_general(q, kb, (((1,), (1,)), ((), ())), preferred_element_type=F32) + bias
    sp = jnp.log(1.0 + jnp.exp(-jnp.abs(z)))
    log_beta = jnp.minimum(z, 0.0) - sp
    log_stay = -jnp.maximum(z, 0.0) - sp
    if mask is not None:
        log_stay = jnp.where(mask, log_stay, 0.0)
    hi = log_stay.astype(BF16)
    lo = (log_stay - hi.astype(F32)).astype(BF16)
    parts = [None] * (z.shape[1] // SB_SUB)
    for s in reversed(range(len(parts))):
        sl = slice(s * SB_SUB, (s + 1) * SB_SUB)
        parts[s] = (carry + jnp.dot(hi[:, sl], tri, preferred_element_type=F32)
                    + jnp.dot(lo[:, sl], tri, preferred_element_type=F32))
        carry = carry + jnp.sum(log_stay[:, sl], axis=1, keepdims=True)
    between = parts[0] if len(parts) == 1 else jnp.concatenate(parts, axis=1)
    att = jnp.exp(log_beta + between)
    if mask is not None:
        att = jnp.where(mask, att, 0.0)
    acc = acc + jnp.dot(att.astype(BF16), vb, preferred_element_type=F32)
    return carry, acc


def _sb_prompt_body(bias_ref, q_ref, k_ref, v_ref, o_ref, *, tq):
    h = pl.program_id(1)
    i = pl.program_id(2)
    bias = bias_ref[h]
    q = (q_ref[...] * SB_SCALE).astype(BF16)
    tri, _, _ = _tri_and_mask(SB_SUB)

    def blk(j, rows, width, carry, acc, mask):
        off = pl.multiple_of(j * tq, tq)
        kb = k_ref[pl.ds(off, width), :].astype(BF16)
        vb = v_ref[pl.ds(off, width), :].astype(BF16)
        return _sb_block(q[rows], kb, vb, bias, carry, acc, tri, mask)

    row = lax.broadcasted_iota(jnp.int32, (tq, tq), 0)
    col = lax.broadcasted_iota(jnp.int32, (tq, tq), 1)
    carry, acc = blk(i, slice(None), tq, jnp.zeros((tq, 1), F32), jnp.zeros((tq, SB_HEAD_DIM), F32),
                     col < row)

    def body(n, c):
        return blk(i - 1 - n, slice(None), tq, c[0], c[1], None)

    carry, acc = lax.fori_loop(0, i, body, (carry, acc))
    o_ref[...] = acc.astype(o_ref.dtype)


def sb_prompt(q, k, v, bias, batch, seq):
    tq = 512
    nq = seq // tq
    seq_spec = pl.BlockSpec((seq, SB_HEAD_DIM), lambda b, h, i: (b, h))
    return pl.pallas_call(
        functools.partial(_sb_prompt_body, tq=tq),
        grid=(batch, SB_HEADS, nq),
        in_specs=[pl.BlockSpec(memory_space=pltpu.SMEM),
                  pl.BlockSpec((tq, SB_HEAD_DIM), lambda b, h, i: (b * nq + i, h)),
                  seq_spec, seq_spec],
        out_specs=pl.BlockSpec((tq, SB_HEAD_DIM), lambda b, h, i: (b * nq + i, h)),
        out_shape=jax.ShapeDtypeStruct((batch * seq, SB_WIDTH), BF16),
        compiler_params=_params("parallel", "parallel", "arbitrary"),
        name="sb_prompt",
    )(bias, q, k, v)


SB_PAGES_PER_STEP = 8


def _sb_sample_body(pt_ref, qbd_ref, bias_ref, kn_ref, vn_ref, *rest, t_new):
    npg = SB_PAGES_PER_STEP
    k_refs, v_refs = rest[:npg], rest[npg:2 * npg]
    o_ref, carry_ref, acc_ref = rest[2 * npg:]
    j = pl.program_id(1)
    rows = qbd_ref.shape[0]
    q = qbd_ref[...]
    bias = bias_ref[...]
    tri, _, _ = _tri_and_mask(SB_SUB)

    @pl.when(j == 0)
    def _():
        row = lax.broadcasted_iota(jnp.int32, (rows, PAGE_SIZE), 0)
        col = lax.broadcasted_iota(jnp.int32, (rows, PAGE_SIZE), 1)
        mask = col < (row % t_new)
        carry, acc = _sb_block(q, kn_ref[...].astype(BF16), vn_ref[...].astype(BF16), bias[:, :PAGE_SIZE],
                               jnp.zeros((rows, 1), F32), jnp.zeros(acc_ref.shape, F32), tri, mask)
        carry_ref[...] = carry
        acc_ref[...] = acc

    def pages(refs):
        return jnp.concatenate(
            [jnp.concatenate([r[pl.ds(h, PAGE_SIZE, stride=SB_HEADS), :].astype(BF16)
                              for h in range(SB_HEADS)], axis=1) for r in refs], axis=0)

    @pl.when(j > 0)
    def _():
        carry, acc = _sb_block(q, pages(k_refs), pages(v_refs), bias,
                               carry_ref[...], acc_ref[...], tri, None)
        carry_ref[...] = carry
        acc_ref[...] = acc

    @pl.when(j == pl.num_programs(1) - 1)
    def _():
        o_ref[...] = acc_ref[...]


def sb_sample(q, k_new, v_new, bias, cache_k, cache_v, layer, page_table):
    bd, n_pages = page_table.shape
    t_new = q.shape[0] // bd
    rows = SB_HEADS * t_new
    n_pool = cache_k.shape[1]
    npg = SB_PAGES_PER_STEP
    q4 = (q * SB_SCALE).reshape(bd, t_new, SB_HEADS, SB_HEAD_DIM).transpose(0, 2, 1, 3)
    eye = jnp.eye(SB_HEADS, dtype=F32)
    qbd = (q4[:, :, :, None, :] * eye[None, :, None, :, None]).reshape(bd, rows, SB_WIDTH).astype(BF16)
    bias_rows = jnp.broadcast_to(jnp.repeat(bias, t_new)[:, None], (rows, npg * PAGE_SIZE))
    pad = ((0, 0), (0, PAGE_SIZE - t_new), (0, 0))
    kn = jnp.pad(k_new.reshape(bd, t_new, SB_WIDTH), pad)
    vn = jnp.pad(v_new.reshape(bd, t_new, SB_WIDTH), pad)
    page_rows = PAGE_SIZE * SB_HEADS
    kc = cache_k.reshape(cache_k.shape[0], n_pool, page_rows, SB_HEAD_DIM)
    vc = cache_v.reshape(cache_v.shape[0], n_pool, page_rows, SB_HEAD_DIM)

    def page_spec(m):
        def page_map(b, j, pt):
            return (layer, pt[(b + 1) * n_pages - npg * jnp.maximum(j, 1) + m], 0, 0)
        return pl.BlockSpec((None, None, page_rows, SB_HEAD_DIM), page_map)

    page_specs = [page_spec(m) for m in range(npg)]
    out = pl.pallas_call(
        functools.partial(_sb_sample_body, t_new=t_new),
        grid_spec=pltpu.PrefetchScalarGridSpec(
            num_scalar_prefetch=1,
            grid=(bd, n_pages // npg + 1),
            in_specs=[pl.BlockSpec((None, rows, SB_WIDTH), lambda b, j, pt: (b, 0, 0)),
                      pl.BlockSpec((rows, npg * PAGE_SIZE), lambda b, j, pt: (0, 0)),
                      pl.BlockSpec((None, PAGE_SIZE, SB_WIDTH), lambda b, j, pt: (b, 0, 0)),
                      pl.BlockSpec((None, PAGE_SIZE, SB_WIDTH), lambda b, j, pt: (b, 0, 0))]
            + page_specs + page_specs,
            out_specs=pl.BlockSpec((None, rows, SB_WIDTH), lambda b, j, pt: (b, 0, 0)),
            scratch_shapes=[pltpu.VMEM((rows, 1), F32), pltpu.VMEM((rows, SB_WIDTH), F32)]),
        out_shape=jax.ShapeDtypeStruct((bd, rows, SB_WIDTH), F32),
        compiler_params=_params("parallel", "arbitrary"),
        name="sb_sample",
    )(page_table.reshape(-1), qbd, bias_rows, kn, vn, *([kc] * npg), *([vc] * npg))
    o5 = out.reshape(bd, SB_HEADS, t_new, SB_HEADS, SB_HEAD_DIM)
    idx = jnp.arange(SB_HEADS)
    diag = o5[:, idx, :, idx, :]
    return diag.transpose(1, 2, 0, 3).reshape(bd * t_new, SB_WIDTH).astype(BF16)


def _split_bf16(x):
    hi = x.astype(BF16)
    lo = (x - hi.astype(F32)).astype(BF16)
    return hi, lo


def _dot3(a, b_hi, b_lo):
    a_hi, a_lo = _split_bf16(a)
    return (jnp.dot(a_hi, b_hi, preferred_element_type=F32)
            + jnp.dot(a_hi, b_lo, preferred_element_type=F32)
            + jnp.dot(a_lo, b_hi, preferred_element_type=F32))


def _head_sum(x, ones_bd):
    hi, lo = _split_bf16(x)
    return (jnp.dot(hi, ones_bd, preferred_element_type=F32)
            + jnp.dot(lo, ones_bd, preferred_element_type=F32))


def _rw_prep_body(zb_ref, halo_ref, prev_ref, mu_ref, w0_ref, a0_ref, kk_ref_w, ka_ref, rk_ref,
                  w2h_ref, w2l_ref, a2h_ref, a2l_ref, g2h_ref, g2l_ref, ones_ref,
                  r_out, w_out, km_out, kk_out, b_out, v_out, g_out, bonus_out):
    i = pl.program_id(1)
    zb = zb_ref[...]
    prev_row = jnp.where(i == 0, prev_ref[...], halo_ref[7:8, :])
    row = lax.broadcasted_iota(jnp.int32, zb.shape, 0)
    shifted = jnp.where(row == 0, prev_row, pltpu.roll(zb, 1, axis=0))
    xz = zb + (shifted - zb) * mu_ref[...]
    c = RW_WIDTH
    r = xz[:, 0:c]
    k = xz[:, c:2 * c]
    v = xz[:, 2 * c:3 * c]
    zw = xz[:, 3 * c:3 * c + 128]
    za = xz[:, 3 * c + 128:3 * c + 256]
    zg = xz[:, 3 * c + 256:3 * c + 512]
    ones_bd = ones_ref[...]
    w_log = -jax.nn.softplus(-(w0_ref[...] + _dot3(jnp.tanh(zw), w2h_ref[...], w2l_ref[...]))) - 0.5
    decay = jnp.exp(-jnp.exp(w_log))
    a = jax.nn.sigmoid(a0_ref[...] + _dot3(za, a2h_ref[...], a2l_ref[...]))
    g = _dot3(jax.nn.sigmoid(zg), g2h_ref[...], g2l_ref[...])
    kk = k * kk_ref_w[...]
    kk = kk / jnp.maximum(jnp.sqrt(_head_sum(kk * kk, ones_bd)), 1e-12)
    k_mod = k * (1.0 + (a - 1.0) * ka_ref[...])
    bonus = _head_sum(r * k_mod * rk_ref[...], ones_bd) * v
    r_out[...] = r
    w_out[...] = decay
    km_out[...] = k_mod
    kk_out[...] = kk
    b_out[...] = kk * a
    v_out[...] = v
    g_out[...] = g
    bonus_out[...] = bonus


def rw_prep(zb, z_prev, p, batch, seq):
    tt = min(seq, 256)
    nt = seq // tt
    rows = batch * seq
    hb = tt // 8

    def row_spec(width):
        return pl.BlockSpec((tt, width), lambda b, i: (b * nt + i, 0))

    def const_spec(shape):
        return pl.BlockSpec(shape, lambda b, i: (0,) * len(shape))

    out_sds = jax.ShapeDtypeStruct((rows, RW_WIDTH), F32)
    return pl.pallas_call(
        _rw_prep_body,
        grid=(batch, nt),
        in_specs=[row_spec(RW_PAD),
                  pl.BlockSpec((8, RW_PAD), lambda b, i: (jnp.maximum((b * nt + i) * hb - 1, 0), 0)),
                  pl.BlockSpec((None, 1, RW_PAD), lambda b, i: (b, 0, 0)),
                  const_spec((1, RW_PAD)),
                  const_spec((1, RW_WIDTH)), const_spec((1, RW_WIDTH)), const_spec((1, RW_WIDTH)),
                  const_spec((1, RW_WIDTH)), const_spec((1, RW_WIDTH)),
                  const_spec((128, RW_WIDTH)), const_spec((128, RW_WIDTH)),
                  const_spec((128, RW_WIDTH)), const_spec((128, RW_WIDTH)),
                  const_spec((256, RW_WIDTH)), const_spec((256, RW_WIDTH)),
                  const_spec((RW_WIDTH, RW_WIDTH))],
        out_specs=[row_spec(RW_WIDTH)] * 8,
        out_shape=[out_sds] * 8,
        compiler_params=_params("parallel", "arbitrary"),
        name="rw_prep",
    )(zb, zb, z_prev.reshape(batch, 1, RW_PAD), p["mu"], p["w0"], p["a0"], p["k_k"], p["k_a"], p["r_k"],
      p["w2h"], p["w2l"], p["a2h"], p["a2l"], p["g2h"], p["g2l"], p["ones_bd"])


def _rw_scan_body(kk_ref, w_ref, b_ref, km_ref, r_ref, v_ref, s0_ref, o_ref, s_ref, *, nv, tc):
    @pl.when(pl.program_id(0) == 0)
    def _():
        s_ref[...] = s0_ref[...]

    def step(t, _):
        for vp in range(nv):
            s = s_ref[vp]
            sa = -jnp.sum(s * kk_ref[t], axis=0, keepdims=True)
            s = s * w_ref[t] + sa * b_ref[t] + v_ref[t, vp:vp + 1, :] * km_ref[t]
            s_ref[vp] = s
            o_ref[t, vp:vp + 1, :] = jnp.sum(s * r_ref[t], axis=0, keepdims=True)
        return 0

    lax.fori_loop(0, tc, step, 0)


def rw_scan(kk, w, bm, km, r, v, s0):
    t_len, kdim, lanes = kk.shape
    nv = v.shape[1]
    tc = min(t_len, 16)
    op_spec = pl.BlockSpec((tc, kdim, lanes), lambda i: (i, 0, 0))
    v_spec = pl.BlockSpec((tc, nv, lanes), lambda i: (i, 0, 0))
    s_spec = pl.BlockSpec((nv, kdim, lanes), lambda i: (0, 0, 0))
    return pl.pallas_call(
        functools.partial(_rw_scan_body, nv=nv, tc=tc),
        grid=(t_len // tc,),
        in_specs=[op_spec] * 5 + [v_spec, s_spec],
        out_specs=[v_spec, s_spec],
        out_shape=[jax.ShapeDtypeStruct((t_len, nv, lanes), F32),
                   jax.ShapeDtypeStruct((nv, kdim, lanes), F32)],
        compiler_params=_params("arbitrary"),
        name="rw_scan",
    )(kk, w, bm, km, r, v, s0)


def _to_scan_body(x_ref, o_ref, xt_ref, *, value_rows):
    nb = x_ref.shape[0]
    nbh = nb * RW_HEADS
    for b in range(nb):
        xt_ref[b * RW_WIDTH:(b + 1) * RW_WIDTH, :] = x_ref[b].T
    if value_rows:
        for vp in range(RW_HEAD_DIM // 2):
            a0 = xt_ref[pl.ds(2 * vp, nbh, stride=RW_HEAD_DIM), :]
            a1 = xt_ref[pl.ds(2 * vp + 1, nbh, stride=RW_HEAD_DIM), :]
            o_ref[:, vp, :] = jnp.concatenate([a0, a1], axis=0).T
    else:
        for c in range(RW_HEAD_DIM):
            a = xt_ref[pl.ds(c, nbh, stride=RW_HEAD_DIM), :]
            o_ref[:, c, :] = jnp.concatenate([a, a], axis=0).T


def to_scan(x, batch, seq, value_rows):
    tt = 128
    rows_out = RW_HEAD_DIM // 2 if value_rows else RW_HEAD_DIM
    return pl.pallas_call(
        functools.partial(_to_scan_body, value_rows=value_rows),
        grid=(seq // tt,),
        in_specs=[pl.BlockSpec((batch, tt, RW_WIDTH), lambda i: (0, i, 0))],
        out_specs=pl.BlockSpec((tt, rows_out, 128), lambda i: (i, 0, 0)),
        out_shape=jax.ShapeDtypeStruct((seq, rows_out, 128), F32),
        scratch_shapes=[pltpu.VMEM((batch * RW_WIDTH, tt), F32)],
        compiler_params=_params("parallel"),
        name="to_scan",
    )(x.reshape(batch, seq, RW_WIDTH))


def _from_scan_body(o_ref, x_ref, xt_ref):
    nb = x_ref.shape[0]
    nbh = nb * RW_HEADS
    for vp in range(RW_HEAD_DIM // 2):
        a = o_ref[:, vp, :].T
        xt_ref[pl.ds(2 * vp, nbh, stride=RW_HEAD_DIM), :] = a[:nbh]
        xt_ref[pl.ds(2 * vp + 1, nbh, stride=RW_HEAD_DIM), :] = a[nbh:]
    for b in range(nb):
        x_ref[b] = xt_ref[b * RW_WIDTH:(b + 1) * RW_WIDTH, :].T


def from_scan(o, batch, seq):
    tt = 128
    out = pl.pallas_call(
        _from_scan_body,
        grid=(seq // tt,),
        in_specs=[pl.BlockSpec((tt, RW_HEAD_DIM // 2, 128), lambda i: (i, 0, 0))],
        out_specs=pl.BlockSpec((batch, tt, RW_WIDTH), lambda i: (0, i, 0)),
        out_shape=jax.ShapeDtypeStruct((batch, seq, RW_WIDTH), F32),
        scratch_shapes=[pltpu.VMEM((batch * RW_WIDTH, tt), F32)],
        compiler_params=_params("parallel"),
        name="from_scan",
    )(o)
    return out.reshape(batch * seq, RW_WIDTH)


def _rw_post_body(o_ref, bonus_ref, g_ref, lw_ref, lb_ref, ones_ref, out_ref):
    o = o_ref[...]
    ones_bd = ones_ref[...]
    inv = 1.0 / RW_HEAD_DIM
    mu = _head_sum(o, ones_bd) * inv
    d = o - mu
    var = _head_sum(d * d, ones_bd) * inv
    y = d * lax.rsqrt(var + GN_EPS) * lw_ref[...] + lb_ref[...]
    out_ref[...] = ((y + bonus_ref[...]) * g_ref[...]).astype(out_ref.dtype)


def rw_post(o, bonus, g, lnx_w, lnx_b, ones_bd):
    rows = o.shape[0]
    tt = min(rows, 512)
    row_spec = pl.BlockSpec((tt, RW_WIDTH), lambda i: (i, 0))
    vec_spec = pl.BlockSpec((1, RW_WIDTH), lambda i: (0, 0))
    return pl.pallas_call(
        _rw_post_body,
        grid=(rows // tt,),
        in_specs=[row_spec, row_spec, row_spec, vec_spec, vec_spec,
                  pl.BlockSpec((RW_WIDTH, RW_WIDTH), lambda i: (0, 0))],
        out_specs=row_spec,
        out_shape=jax.ShapeDtypeStruct((rows, RW_WIDTH), BF16),
        compiler_params=_params("parallel"),
        name="rw_post",
    )(o, bonus, g, lnx_w, lnx_b, ones_bd)


def rwkv7(zb, z_prev, wkv0, p, batch, seq):
    r, w, km, kk, bm, v, g, bonus = rw_prep(zb, z_prev, p, batch, seq)
    bh = batch * RW_HEADS
    vpar = 128 // bh
    nv = RW_HEAD_DIM // vpar
    s0 = wkv0.reshape(batch, RW_HEADS, nv, vpar, RW_HEAD_DIM).transpose(2, 4, 3, 0, 1).reshape(
        nv, RW_HEAD_DIM, 128)
    if vpar == 2 and seq % 128 == 0:
        ops = [to_scan(x, batch, seq, False) for x in (kk, w, bm, km, r)]
        o_l, s_l = rw_scan(*ops, to_scan(v, batch, seq, True), s0)
        o = from_scan(o_l, batch, seq)
    else:

        def key_layout(x):
            x = x.reshape(batch, seq, RW_HEADS, RW_HEAD_DIM).transpose(1, 3, 0, 2)
            return jnp.concatenate([x.reshape(seq, RW_HEAD_DIM, bh)] * vpar, axis=-1)

        v_l = v.reshape(batch, seq, RW_HEADS, nv, vpar).transpose(1, 3, 4, 0, 2).reshape(seq, nv, 128)
        o_l, s_l = rw_scan(key_layout(kk), key_layout(w), key_layout(bm), key_layout(km), key_layout(r),
                           v_l, s0)
        o = o_l.reshape(seq, nv, vpar, batch, RW_HEADS).transpose(3, 0, 4, 1, 2).reshape(
            batch * seq, RW_WIDTH)
    s_fin = s_l.reshape(nv, RW_HEAD_DIM, vpar, batch, RW_HEADS).transpose(3, 4, 0, 2, 1).reshape(
        batch, RW_HEADS, RW_HEAD_DIM, RW_HEAD_DIM)
    out = rw_post(o, bonus, g, p["lnx_w"], p["lnx_b"], p["ones_bd"])
    return out, s_fin


def _gmlp_body(u_ref, v_ref, lnw_ref, lnb_ref, ws_ref, bs_ref, o_ref, *vn_refs, inner):
    v = v_ref[...]
    mu = jnp.mean(v, axis=-1, keepdims=True)
    var = jnp.mean(jnp.square(v - mu), axis=-1, keepdims=True)
    vn = (v - mu) * lax.rsqrt(var + LN_EPS) * lnw_ref[...] + lnb_ref[...]
    if vn_refs:
        vn_refs[0][...] = vn
    n = v.shape[0]
    row = lax.broadcasted_iota(jnp.int32, (n, n), 0)
    col = lax.broadcasted_iota(jnp.int32, (n, n), 1)
    causal = (col <= row) & (col >= row - row % inner)
    for gi in range(C_GROUPS):
        sl = slice(gi * C_GROUP_DIM, (gi + 1) * C_GROUP_DIM)
        wg = jnp.where(causal, ws_ref[gi], 0.0).astype(BF16)
        mixed = jnp.dot(wg, vn[:, sl].astype(BF16), preferred_element_type=F32) + bs_ref[:, gi:gi + 1]
        o_ref[:, sl] = (u_ref[:, sl] * mixed).astype(o_ref.dtype)


def gmlp_gate(z, ln_w, ln_b, ws, bs_t, rows_per_step, inner, want_v):
    rows = z.shape[0]
    n = rows_per_step
    out_shape = [jax.ShapeDtypeStruct((rows, C_WIDTH), BF16)]
    out_specs = [pl.BlockSpec((n, C_WIDTH), lambda i: (i, 0))]
    if want_v:
        out_shape.append(jax.ShapeDtypeStruct((rows, C_WIDTH), F32))
        out_specs.append(pl.BlockSpec((n, C_WIDTH), lambda i: (i, 0)))
    res = pl.pallas_call(
        functools.partial(_gmlp_body, inner=inner),
        grid=(rows // n,),
        in_specs=[pl.BlockSpec((n, C_WIDTH), lambda i: (i, 0)),
                  pl.BlockSpec((n, C_WIDTH), lambda i: (i, 1)),
                  pl.BlockSpec((1, C_WIDTH), lambda i: (0, 0)),
                  pl.BlockSpec((1, C_WIDTH), lambda i: (0, 0)),
                  pl.BlockSpec((C_GROUPS, n, n), lambda i: (0, 0, 0)),
                  pl.BlockSpec((n, C_GROUPS), lambda i: (0, 0))],
        out_specs=out_specs,
        out_shape=out_shape,
        compiler_params=_params("parallel"),
        name="gmlp_gate",
    )(z, z, ln_w, ln_b, ws, bs_t)
    return res if want_v else (res[0], None)


def _conv_gate_body(ug_ref, uv_ref, hg_ref, hv_ref, pg_ref, pv_ref, wg_ref, wv_ref, bg_ref, bv_ref, o_ref):
    i = pl.program_id(1)

    def conv(up_ref, halo_ref, prev_ref, w_ref, b_ref):
        h2 = jnp.where(i == 0, prev_ref[...], halo_ref[6:8, :])
        return _conv3(up_ref[...], h2, w_ref, b_ref)

    gate = conv(ug_ref, hg_ref, pg_ref, wg_ref, bg_ref)
    val = conv(uv_ref, hv_ref, pv_ref, wv_ref, bv_ref)
    o_ref[...] = (jax.nn.silu(gate) * val).astype(o_ref.dtype)


def conv_gate(up, prev, conv_w, conv_b, batch, seq):
    tt = min(seq, 256)
    nt = seq // tt
    hb = tt // 8
    tn = 1024
    nj = FFN_DIM // tn

    def halo_map(off):
        return lambda b, i, j: (jnp.maximum((b * nt + i) * hb - 1, 0), j + off)

    def col_spec(shape, off):
        return pl.BlockSpec(shape, lambda b, i, j: (0, j + off))

    return pl.pallas_call(
        _conv_gate_body,
        grid=(batch, nt, nj),
        in_specs=[pl.BlockSpec((tt, tn), lambda b, i, j: (b * nt + i, j)),
                  pl.BlockSpec((tt, tn), lambda b, i, j: (b * nt + i, j + nj)),
                  pl.BlockSpec((8, tn), halo_map(0)),
                  pl.BlockSpec((8, tn), halo_map(nj)),
                  pl.BlockSpec((None, 2, tn), lambda b, i, j: (b, 0, j)),
                  pl.BlockSpec((None, 2, tn), lambda b, i, j: (b, 0, j + nj)),
                  col_spec((CONV_W, tn), 0), col_spec((CONV_W, tn), nj),
                  col_spec((1, tn), 0), col_spec((1, tn), nj)],
        out_specs=pl.BlockSpec((tt, tn), lambda b, i, j: (b * nt + i, j)),
        out_shape=jax.ShapeDtypeStruct((batch * seq, FFN_DIM), BF16),
        compiler_params=_params("parallel", "parallel", "parallel"),
        name="conv_gate",
    )(up, up, up, up, prev, prev, conv_w, conv_w, conv_b, conv_b)


def _conv3(up, h2, w_ref, b_ref):
    row = lax.broadcasted_iota(jnp.int32, up.shape, 0)
    x1 = jnp.where(row == 0, h2[1:2, :], pltpu.roll(up, 1, axis=0))
    x2 = jnp.where(row == 0, h2[0:1, :], jnp.where(row == 1, h2[1:2, :], pltpu.roll(up, 2, axis=0)))
    return b_ref[...] + (x2 * w_ref[0:1, :] + x1 * w_ref[1:2, :] + up * w_ref[2:3, :])


def _up_conv_body(x_ref, xh_ref, wg_ref, wv_ref, pg_ref, pv_ref, cwg_ref, cwv_ref, cbg_ref, cbv_ref,
                  o_ref, tg_ref, tv_ref, wgb_ref, wvb_ref, *, tiles_per_batch):
    i = pl.program_id(1)
    first = i % tiles_per_batch == 0
    x = x_ref[...]
    xh = xh_ref[...]
    tm = x.shape[0]

    @pl.when(i == 0)
    def _():
        wgb_ref[...] = wg_ref[...].astype(BF16)
        wvb_ref[...] = wv_ref[...].astype(BF16)

    def half(w_ref, prev_ref, cw_ref, cb_ref, tail_ref):
        w = w_ref[...]
        up = jnp.dot(x, w, preferred_element_type=F32)
        up_before = jnp.dot(xh, w, preferred_element_type=F32)
        h2 = jnp.where(first, prev_ref[...], up_before[14:16, :])
        tail_ref[...] = up[tm - 8:, :]
        return _conv3(up, h2, cw_ref, cb_ref)

    gate = half(wgb_ref, pg_ref, cwg_ref, cbg_ref, tg_ref)
    val = half(wvb_ref, pv_ref, cwv_ref, cbv_ref, tv_ref)
    o_ref[...] = (jax.nn.silu(gate) * val).astype(o_ref.dtype)


def ffn_up_conv(x, w_up, prev, conv_w, conv_b, batch, seq):
    assert seq % 1024 == 0
    m, k = x.shape
    tm, tn = 1024, 512
    tiles_per_batch = seq // tm
    nj = FFN_DIM // tn
    hb = tm // 16

    def col_spec(shape, off):
        return pl.BlockSpec(shape, lambda j, i: (0, j + off))

    def batch_spec(rows, off):
        return pl.BlockSpec((None, rows, tn), lambda j, i: (i // tiles_per_batch, 0, j + off))

    hid, tail_g, tail_v = pl.pallas_call(
        functools.partial(_up_conv_body, tiles_per_batch=tiles_per_batch),
        grid=(nj, m // tm),
        in_specs=[pl.BlockSpec((tm, k), lambda j, i: (i, 0)),
                  pl.BlockSpec((16, k), lambda j, i: (jnp.maximum(i * hb - 1, 0), 0)),
                  col_spec((k, tn), 0), col_spec((k, tn), nj),
                  batch_spec(2, 0), batch_spec(2, nj),
                  col_spec((CONV_W, tn), 0), col_spec((CONV_W, tn), nj),
                  col_spec((1, tn), 0), col_spec((1, tn), nj)],
        out_specs=[pl.BlockSpec((tm, tn), lambda j, i: (i, j)),
                   pl.BlockSpec((None, 8, tn), lambda j, i: (i, 0, j)),
                   pl.BlockSpec((None, 8, tn), lambda j, i: (i, 0, j))],
        out_shape=[jax.ShapeDtypeStruct((m, FFN_DIM), BF16),
                   jax.ShapeDtypeStruct((m // tm, 8, FFN_DIM), F32),
                   jax.ShapeDtypeStruct((m // tm, 8, FFN_DIM), F32)],
        scratch_shapes=[pltpu.VMEM((k, tn), BF16), pltpu.VMEM((k, tn), BF16)],
        compiler_params=_params("parallel", "arbitrary"),
        name="ffn_up_conv",
    )(x, x, w_up, w_up, prev, prev, conv_w, conv_w, conv_b, conv_b)
    tails = jnp.concatenate([tail_g, tail_v], axis=-1)
    return hid, tails.reshape(batch, tiles_per_batch, 8, 2 * FFN_DIM)[:, -1]


def _pad_cols(x, segs):
    c = 3 * RW_WIDTH
    parts = [x[..., :c]]
    start = c
    for width, padded in segs:
        seg = x[..., start:start + width]
        parts.append(jnp.pad(seg, [(0, 0)] * (x.ndim - 1) + [(0, padded - width)]))
        start += width
    return jnp.concatenate(parts, axis=-1)


_LORA_SEGS = ((LORA_W, 128), (LORA_A, 128), (LORA_G, 256))


def _unpad_cols(x):
    c = 3 * RW_WIDTH
    return jnp.concatenate([x[..., :c], x[..., c:c + LORA_W], x[..., c + 128:c + 128 + LORA_A],
                            x[..., c + 256:c + 256 + LORA_G]], axis=-1)


def _pad_rows(w, padded):
    return jnp.pad(w, ((0, padded - w.shape[0]), (0, 0)))


def kernel(x_prompt, x_sample, cache_k, cache_v, page_table, state_shift, state_wkv, state_conv, norm_mix, norm_ffn, norm_final, w_in_even, sb_bias, mu_shift, w0, w2, a0, a2, g2, k_k, k_a, r_k, lnx_w, lnx_b, w_out_even, w_in_odd, ln_v_w, ln_v_b, w_spatial, b_spatial, w_out_odd, w_up, conv_w, conv_b, w_down):
    bp, sp = x_prompt.shape[:2]
    bs, ts = x_sample.shape[:2]
    xp = x_prompt.reshape(bp * sp, D_MODEL)
    xs = x_sample.reshape(bs * ts, D_MODEL)
    groups = ((bp, sp), (bs, ts))

    head_id = jnp.arange(RW_WIDTH) // RW_HEAD_DIM
    ones_bd = (head_id[:, None] == head_id[None, :]).astype(BF16)

    k_out, v_out, sh_out, wkv_out, cv_out, chunkv_out = ([], []), ([], []), ([], []), ([], []), ([], []), []
    xs_all = [xp, xs]
    for l in range(DEPTH):
        i = l // 2
        h_all = [rmsnorm(x, norm_mix[l], BF16) for x in xs_all]
        if l % 2 == 0:
            w_rw = _pad_cols(w_in_even[i][:, 3 * SB_WIDTH:], _LORA_SEGS)
            q_all, k_all, v_all = (matmul(h_all[0], w_in_even[i], col0=c * SB_WIDTH, n=SB_WIDTH, rider=h_all[1])
                                   for c in range(3))
            zb_all = matmul(h_all[0], w_rw, rider=h_all[1])
            mixed = []
            w2h, w2l = _split_bf16(_pad_rows(w2[i], 128))
            a2h, a2l = _split_bf16(_pad_rows(a2[i], 128))
            g2h, g2l = _split_bf16(_pad_rows(g2[i], 256))
            p = dict(mu=_pad_cols(mu_shift[i][None, :], _LORA_SEGS), w0=w0[i][None, :], a0=a0[i][None, :],
                     k_k=k_k[i][None, :], k_a=k_a[i][None, :], r_k=r_k[i].reshape(1, RW_WIDTH),
                     w2h=w2h, w2l=w2l, a2h=a2h, a2l=a2l, g2h=g2h, g2l=g2l, ones_bd=ones_bd,
                     lnx_w=lnx_w[i][None, :], lnx_b=lnx_b[i][None, :])
            for gi, (nb, nt) in enumerate(groups):
                q_rows, k_rows, v_rows, zb = q_all[gi], k_all[gi], v_all[gi], zb_all[gi]
                if gi == 0:
                    att = sb_prompt(q_rows, k_rows, v_rows, sb_bias[i], nb, nt)
                    z_prev = jnp.zeros((nb, RW_PAD), F32)
                    wkv0 = jnp.zeros((nb, RW_HEADS, RW_HEAD_DIM, RW_HEAD_DIM), F32)
                else:
                    att = sb_sample(q_rows, k_rows, v_rows, sb_bias[i], cache_k, cache_v, i, page_table)
                    z_prev = _pad_cols(state_shift[i], _LORA_SEGS)
                    wkv0 = state_wkv[i]
                rw, s_fin = rwkv7(zb, z_prev, wkv0, p, nb, nt)
                mixed.append(jnp.concatenate([att, rw], axis=-1))
                k_out[gi].append(k_rows.reshape(nb, nt, SB_HEADS, SB_HEAD_DIM))
                v_out[gi].append(v_rows.reshape(nb, nt, SB_HEADS, SB_HEAD_DIM))
                sh_out[gi].append(_unpad_cols(zb.reshape(nb, nt, RW_PAD)[:, -1]))
                wkv_out[gi].append(s_fin)
            xs_all = list(matmul(mixed[0], w_out_even[i], res=xs_all[0], rider=mixed[1], rider_res=xs_all[1]))
        else:
            lnw = ln_v_w[i][None, :]
            lnb = ln_v_b[i][None, :]
            z_all = matmul(h_all[0], w_in_odd[i], act="gelu", rider=h_all[1])
            mixed = []
            for gi, (nb, nt) in enumerate(groups):
                z = z_all[gi]
                if gi == 0:
                    gated, _ = gmlp_gate(z, lnw, lnb, w_spatial[i], b_spatial[i].T, CHUNK, CHUNK, False)
                else:
                    eye = jnp.eye(nb, dtype=F32)
                    ws_s = w_spatial[i][:, :nt, :nt]
                    ws_bd = (eye[None, :, None, :, None] * ws_s[:, None, :, None, :]).reshape(
                        C_GROUPS, nb * nt, nb * nt)
                    bs_t = jnp.tile(b_spatial[i][:, :nt].T, (nb, 1))
                    gated, v_rows = gmlp_gate(z, lnw, lnb, ws_bd, bs_t, nb * nt, nt, True)
                    chunkv_out.append(v_rows.reshape(nb, nt, C_WIDTH))
                mixed.append(gated)
            xs_all = list(matmul(mixed[0], w_out_odd[i], res=xs_all[0], rider=mixed[1], rider_res=xs_all[1]))
        hidden = []
        for gi, (nb, nt) in enumerate(groups):
            hf = rmsnorm(xs_all[gi], norm_ffn[l], BF16)
            if gi == 0:
                prev = jnp.zeros((nb, CONV_W - 1, 2 * FFN_DIM), F32)
                hid, up_tail = ffn_up_conv(hf, w_up[l], prev, conv_w[l], conv_b[l][None, :], nb, nt)
            else:
                prev = state_conv[l]
                up = matmul(hf, w_up[l])
                hid = conv_gate(up, prev, conv_w[l], conv_b[l][None, :], nb, nt)
                up_tail = up.reshape(nb, nt, 2 * FFN_DIM)
            hidden.append(hid)
            cv_out[gi].append(jnp.concatenate([prev, up_tail], axis=1)[:, -(CONV_W - 1):])
        xs_all = list(matmul(hidden[0], w_down[l], res=xs_all[0], rider=hidden[1], rider_res=xs_all[1]))
    y_prompt = rmsnorm(xs_all[0], norm_final, F32).reshape(bp, sp, D_MODEL)
    y_sample = rmsnorm(xs_all[1], norm_final, F32).reshape(bs, ts, D_MODEL)
    st = jnp.stack
    return (y_prompt, y_sample, st(k_out[0]), st(v_out[0]), st(k_out[1]), st(v_out[1]),
            st(sh_out[0]), st(sh_out[1]), st(wkv_out[0]), st(wkv_out[1]),
            st(cv_out[0]), st(cv_out[1]), st(chunkv_out))
```

```python
import functools

import jax
import jax.numpy as jnp
from jax import lax
from jax.experimental import pallas as pl
from jax.experimental.pallas import tpu as pltpu

F32 = jnp.float32
BF16 = jnp.bfloat16

D_MODEL = 2048
DEPTH = 4
PAGE_SIZE = 128
SB_HEAD_DIM = 128
SB_HEADS = 8
SB_WIDTH = 1024
RW_HEAD_DIM = 64
RW_WIDTH = 1024
RW_HEADS = 16
LORA_W = 64
LORA_A = 64
LORA_G = 160
RW_IN = 3 * RW_WIDTH + LORA_W + LORA_A + LORA_G
RW_LORA_PAD = 128 + 128 + 256
RW_PAD = 3 * RW_WIDTH + RW_LORA_PAD
GN_EPS = 64e-5
C_WIDTH = 2 * D_MODEL
C_GROUPS = 8
C_GROUP_DIM = C_WIDTH // C_GROUPS
CHUNK = 128
FFN_DIM = 2 * D_MODEL
CONV_W = 3
RMS_EPS = 1e-6
LN_EPS = 1e-5
SB_SCALE = SB_HEAD_DIM ** -0.5
TIME_SUB = 8

VMEM_LIMIT_BYTES = 52 * 1024 * 1024


def _params(*sem):
    return pltpu.CompilerParams(dimension_semantics=sem, vmem_limit_bytes=VMEM_LIMIT_BYTES)


def _rmsnorm_body(x_ref, g_ref, o_ref):
    x = x_ref[...]
    ms = jnp.mean(x * x, axis=-1, keepdims=True)
    o_ref[...] = (x * lax.rsqrt(ms + RMS_EPS) * g_ref[...]).astype(o_ref.dtype)


def rmsnorm(x, g, out_dtype):
    m, d = x.shape
    tm = min(m, 512)
    return pl.pallas_call(
        _rmsnorm_body,
        grid=(m // tm,),
        in_specs=[pl.BlockSpec((tm, d), lambda i: (i, 0)),
                  pl.BlockSpec((1, d), lambda i: (0, 0))],
        out_specs=pl.BlockSpec((tm, d), lambda i: (i, 0)),
        out_shape=jax.ShapeDtypeStruct((m, d), out_dtype),
        compiler_params=_params("parallel"),
        name="rmsnorm",
    )(x, g.reshape(1, d))


def _matmul_body(*refs, act, n_parts, has_res, has_rider):
    refs = list(refs)
    wb_ref = refs.pop()
    groups = 2 if has_rider else 1
    x_refs = [refs[g * n_parts:(g + 1) * n_parts] for g in range(groups)]
    w_ref = refs[groups * n_parts]
    r_refs = refs[groups * n_parts + 1:][:groups] if has_res else [None] * groups
    o_refs = refs[-groups:]

    def product(g):
        acc, k0 = None, 0
        for x_ref in x_refs[g]:
            kp = x_ref.shape[1]
            part = jnp.dot(x_ref[...], wb_ref[k0:k0 + kp, :], preferred_element_type=F32)
            acc = part if acc is None else acc + part
            k0 += kp
        if act == "gelu":
            acc = jax.nn.gelu(acc)
        if has_res:
            acc = r_refs[g][...] + acc
        o_refs[g][...] = acc.astype(o_refs[g].dtype)

    @pl.when(pl.program_id(1) == 0)
    def _():
        wb_ref[...] = w_ref[...].astype(BF16)
        if has_rider:
            product(1)

    product(0)


def matmul(x, w, layer=None, col0=0, n=None, res=None, act=None, rider=None, rider_res=None, out_dtype=F32):
    xs = list(x) if isinstance(x, (list, tuple)) else [x]
    has_res, has_rider = res is not None, rider is not None
    groups = [xs]
    if has_rider:
        groups.append(list(rider) if isinstance(rider, (list, tuple)) else [rider])
    m = xs[0].shape[0]
    k = sum(p.shape[1] for p in xs)
    n = w.shape[-1] - col0 if n is None else n
    tm = min(m, 1024)
    tn = 1024 if (k <= 2048 and n % 1024 == 0 and col0 % 1024 == 0) else 512
    c0 = col0 // tn
    row_tiled = lambda j, i: (i, 0)
    whole = lambda j, i: (0, 0)
    in_specs, args = [], []
    for g, parts in enumerate(groups):
        for p in parts:
            rows = tm if g == 0 else p.shape[0]
            in_specs.append(pl.BlockSpec((rows, p.shape[1]), row_tiled if g == 0 else whole))
            args.append(p)
    if layer is None:
        in_specs.append(pl.BlockSpec((k, tn), lambda j, i: (0, j + c0)))
    else:
        in_specs.append(pl.BlockSpec((None, k, tn), lambda j, i: (layer, 0, j + c0)))
    args.append(w)
    out_specs = [pl.BlockSpec((tm, tn), lambda j, i: (i, j))]
    out_shape = [jax.ShapeDtypeStruct((m, n), out_dtype)]
    if has_res:
        in_specs.append(pl.BlockSpec((tm, tn), lambda j, i: (i, j)))
        args.append(res)
    if has_rider:
        mr = groups[1][0].shape[0]
        out_specs.append(pl.BlockSpec((mr, tn), lambda j, i: (0, j)))
        out_shape.append(jax.ShapeDtypeStruct((mr, n), F32))
        if has_res:
            in_specs.append(pl.BlockSpec((mr, tn), lambda j, i: (0, j)))
            args.append(rider_res)
    out = pl.pallas_call(
        functools.partial(_matmul_body, act=act, n_parts=len(xs), has_res=has_res, has_rider=has_rider),
        grid=(n // tn, m // tm),
        in_specs=in_specs,
        out_specs=out_specs,
        out_shape=out_shape,
        scratch_shapes=[pltpu.VMEM((k, tn), BF16)],
        compiler_params=_params("parallel", "arbitrary"),
        name="matmul",
    )(*args)
    return tuple(out) if has_rider else out[0]


def _tri_and_mask(n):
    row = lax.broadcasted_iota(jnp.int32, (n, n), 0)
    col = lax.broadcasted_iota(jnp.int32, (n, n), 1)
    tri = jnp.where(row > col, 1.0, 0.0).astype(BF16)
    return tri, row, col


SB_SUB = 128


def _sb_block(q, kb, vb, bias, carry, acc, tri, mask):
    z = lax.dot_general(q, kb, (((1,), (1,)), ((), ())), preferred_element_type=F32) + bias
    sp = jnp.log(1.0 + jnp.exp(-jnp.abs(z)))
    log_beta = jnp.minimum(z, 0.0) - sp
    log_stay = -jnp.maximum(z, 0.0) - sp
    if mask is not None:
        log_stay = jnp.where(mask, log_stay, 0.0)
    hi = log_stay.astype(BF16)
    lo = (log_stay - hi.astype(F32)).astype(BF16)
    parts = [None] * (z.shape[1] // SB_SUB)
    for s in reversed(range(len(parts))):
        sl = slice(s * SB_SUB, (s + 1) * SB_SUB)
        parts[s] = (carry + jnp.dot(hi[:, sl], tri, preferred_element_type=F32)
                    + jnp.dot(lo[:, sl], tri, preferred_element_type=F32))
        carry = carry + jnp.sum(log_stay[:, sl], axis=1, keepdims=True)
    between = parts[0] if len(parts) == 1 else jnp.concatenate(parts, axis=1)
    att = jnp.exp(log_beta + between)
    if mask is not None:
        att = jnp.where(mask, att, 0.0)
    acc = acc + jnp.dot(att.astype(BF16), vb, preferred_element_type=F32)
    return carry, acc


def _sb_prompt_body(bias_ref, q_ref, k_ref, v_ref, o_ref, *, tq):
    h = pl.program_id(1)
    i = pl.program_id(2)
    bias = bias_ref[h]
    q = (q_ref[...] * SB_SCALE).astype(BF16)
    tri, _, _ = _tri_and_mask(SB_SUB)

    def blk(j, rows, width, carry, acc, mask):
        off = pl.multiple_of(j * tq, tq)
        kb = k_ref[pl.ds(off, width), :].astype(BF16)
        vb = v_ref[pl.ds(off, width), :].astype(BF16)
        return _sb_block(q[rows], kb, vb, bias, carry, acc, tri, mask)

    row = lax.broadcasted_iota(jnp.int32, (tq, tq), 0)
    col = lax.broadcasted_iota(jnp.int32, (tq, tq), 1)
    carry, acc = blk(i, slice(None), tq, jnp.zeros((tq, 1), F32), jnp.zeros((tq, SB_HEAD_DIM), F32),
                     col < row)

    def body(n, c):
        return blk(i - 1 - n, slice(None), tq, c[0], c[1], None)

    carry, acc = lax.fori_loop(0, i, body, (carry, acc))
    o_ref[...] = acc.astype(o_ref.dtype)


def sb_prompt(q, k, v, bias, batch, seq):
    tq = 512
    nq = seq // tq
    seq_spec = pl.BlockSpec((seq, SB_HEAD_DIM), lambda b, h, i: (b, h))
    return pl.pallas_call(
        functools.partial(_sb_prompt_body, tq=tq),
        grid=(batch, SB_HEADS, nq),
        in_specs=[pl.BlockSpec(memory_space=pltpu.SMEM),
                  pl.BlockSpec((tq, SB_HEAD_DIM), lambda b, h, i: (b * nq + i, h)),
                  seq_spec, seq_spec],
        out_specs=pl.BlockSpec((tq, SB_HEAD_DIM), lambda b, h, i: (b * nq + i, h)),
        out_shape=jax.ShapeDtypeStruct((batch * seq, SB_WIDTH), BF16),
        compiler_params=_params("parallel", "parallel", "arbitrary"),
        name="sb_prompt",
    )(bias, q, k, v)


SB_PAGES_PER_STEP = 8


def _sb_sample_body(pt_ref, qbd_ref, bias_ref, kn_ref, vn_ref, *rest, t_new):
    npg = SB_PAGES_PER_STEP
    k_refs, v_refs = rest[:npg], rest[npg:2 * npg]
    o_ref, carry_ref, acc_ref = rest[2 * npg:]
    j = pl.program_id(1)
    rows = qbd_ref.shape[0]
    q = qbd_ref[...]
    bias = bias_ref[...]
    tri, _, _ = _tri_and_mask(SB_SUB)

    @pl.when(j == 0)
    def _():
        row = lax.broadcasted_iota(jnp.int32, (rows, PAGE_SIZE), 0)
        col = lax.broadcasted_iota(jnp.int32, (rows, PAGE_SIZE), 1)
        mask = col < (row % t_new)
        carry, acc = _sb_block(q, kn_ref[...].astype(BF16), vn_ref[...].astype(BF16), bias[:, :PAGE_SIZE],
                               jnp.zeros((rows, 1), F32), jnp.zeros(acc_ref.shape, F32), tri, mask)
        carry_ref[...] = carry
        acc_ref[...] = acc

    def pages(refs):
        return jnp.concatenate(
            [jnp.concatenate([r[pl.ds(h, PAGE_SIZE, stride=SB_HEADS), :].astype(BF16)
                              for h in range(SB_HEADS)], axis=1) for r in refs], axis=0)

    @pl.when(j > 0)
    def _():
        carry, acc = _sb_block(q, pages(k_refs), pages(v_refs), bias,
                               carry_ref[...], acc_ref[...], tri, None)
        carry_ref[...] = carry
        acc_ref[...] = acc

    @pl.when(j == pl.num_programs(1) - 1)
    def _():
        o_ref[...] = acc_ref[...]


def sb_sample(q, k_new, v_new, bias, cache_k, cache_v, layer, page_table):
    bd, n_pages = page_table.shape
    t_new = q.shape[0] // bd
    rows = SB_HEADS * t_new
    n_pool = cache_k.shape[1]
    npg = SB_PAGES_PER_STEP
    q4 = (q * SB_SCALE).reshape(bd, t_new, SB_HEADS, SB_HEAD_DIM).transpose(0, 2, 1, 3)
    eye = jnp.eye(SB_HEADS, dtype=F32)
    qbd = (q4[:, :, :, None, :] * eye[None, :, None, :, None]).reshape(bd, rows, SB_WIDTH).astype(BF16)
    bias_rows = jnp.broadcast_to(jnp.repeat(bias, t_new)[:, None], (rows, npg * PAGE_SIZE))
    pad = ((0, 0), (0, PAGE_SIZE - t_new), (0, 0))
    kn = jnp.pad(k_new.reshape(bd, t_new, SB_WIDTH), pad)
    vn = jnp.pad(v_new.reshape(bd, t_new, SB_WIDTH), pad)
    page_rows = PAGE_SIZE * SB_HEADS
    kc = cache_k.reshape(cache_k.shape[0], n_pool, page_rows, SB_HEAD_DIM)
    vc = cache_v.reshape(cache_v.shape[0], n_pool, page_rows, SB_HEAD_DIM)

    def page_spec(m):
        def page_map(b, j, pt):
            return (layer, pt[(b + 1) * n_pages - npg * jnp.maximum(j, 1) + m], 0, 0)
        return pl.BlockSpec((None, None, page_rows, SB_HEAD_DIM), page_map)

    page_specs = [page_spec(m) for m in range(npg)]
    out = pl.pallas_call(
        functools.partial(_sb_sample_body, t_new=t_new),
        grid_spec=pltpu.PrefetchScalarGridSpec(
            num_scalar_prefetch=1,
            grid=(bd, n_pages // npg + 1),
            in_specs=[pl.BlockSpec((None, rows, SB_WIDTH), lambda b, j, pt: (b, 0, 0)),
                      pl.BlockSpec((rows, npg * PAGE_SIZE), lambda b, j, pt: (0, 0)),
                      pl.BlockSpec((None, PAGE_SIZE, SB_WIDTH), lambda b, j, pt: (b, 0, 0)),
                      pl.BlockSpec((None, PAGE_SIZE, SB_WIDTH), lambda b, j, pt: (b, 0, 0))]
            + page_specs + page_specs,
            out_specs=pl.BlockSpec((None, rows, SB_WIDTH), lambda b, j, pt: (b, 0, 0)),
            scratch_shapes=[pltpu.VMEM((rows, 1), F32), pltpu.VMEM((rows, SB_WIDTH), F32)]),
        out_shape=jax.ShapeDtypeStruct((bd, rows, SB_WIDTH), F32),
        compiler_params=_params("parallel", "arbitrary"),
        name="sb_sample",
    )(page_table.reshape(-1), qbd, bias_rows, kn, vn, *([kc] * npg), *([vc] * npg))
    o5 = out.reshape(bd, SB_HEADS, t_new, SB_HEADS, SB_HEAD_DIM)
    idx = jnp.arange(SB_HEADS)
    diag = o5[:, idx, :, idx, :]
    return diag.transpose(1, 2, 0, 3).reshape(bd * t_new, SB_WIDTH).astype(BF16)


def _split_bf16(x):
    hi = x.astype(BF16)
    lo = (x - hi.astype(F32)).astype(BF16)
    return hi, lo


def _dot3(a, b_hi, b_lo):
    a_hi, a_lo = _split_bf16(a)
    return (jnp.dot(a_hi, b_hi, preferred_element_type=F32)
            + jnp.dot(a_hi, b_lo, preferred_element_type=F32)
            + jnp.dot(a_lo, b_hi, preferred_element_type=F32))


def _head_sum(x, ones_bd):
    hi, lo = _split_bf16(x)
    return (jnp.dot(hi, ones_bd, preferred_element_type=F32)
            + jnp.dot(lo, ones_bd, preferred_element_type=F32))


def _rw_prep_body(zm_ref, zl_ref, hm_ref, hl_ref, prev_ref, mu_ref, w0_ref, a0_ref, kk_ref_w, ka_ref, rk_ref,
                  w2h_ref, w2l_ref, a2h_ref, a2l_ref, g2h_ref, g2l_ref, ones_ref,
                  r_out, w_out, km_out, kk_out, b_out, v_out, g_out, bonus_out):
    i = pl.program_id(1)
    zb = jnp.concatenate([zm_ref[...], zl_ref[...]], axis=1)
    halo = jnp.concatenate([hm_ref[7:8, :], hl_ref[7:8, :]], axis=1)
    prev_row = jnp.where(i == 0, prev_ref[...], halo)
    row = lax.broadcasted_iota(jnp.int32, zb.shape, 0)
    shifted = jnp.where(row == 0, prev_row, pltpu.roll(zb, 1, axis=0))
    xz = zb + (shifted - zb) * mu_ref[...]
    c = RW_WIDTH
    r = xz[:, 0:c]
    k = xz[:, c:2 * c]
    v = xz[:, 2 * c:3 * c]
    zw = xz[:, 3 * c:3 * c + 128]
    za = xz[:, 3 * c + 128:3 * c + 256]
    zg = xz[:, 3 * c + 256:3 * c + 512]
    ones_bd = ones_ref[...]
    w_log = -jax.nn.softplus(-(w0_ref[...] + _dot3(jnp.tanh(zw), w2h_ref[...], w2l_ref[...]))) - 0.5
    decay = jnp.exp(-jnp.exp(w_log))
    a = jax.nn.sigmoid(a0_ref[...] + _dot3(za, a2h_ref[...], a2l_ref[...]))
    g = _dot3(jax.nn.sigmoid(zg), g2h_ref[...], g2l_ref[...])
    kk = k * kk_ref_w[...]
    kk = kk / jnp.maximum(jnp.sqrt(_head_sum(kk * kk, ones_bd)), 1e-12)
    k_mod = k * (1.0 + (a - 1.0) * ka_ref[...])
    bonus = _head_sum(r * k_mod * rk_ref[...], ones_bd) * v
    r_out[...] = r
    w_out[...] = decay
    km_out[...] = k_mod
    kk_out[...] = kk
    b_out[...] = kk * a
    v_out[...] = v
    g_out[...] = g
    bonus_out[...] = bonus


def rw_prep(z_main, z_lora, z_prev, p, batch, seq):
    tt = min(seq, 256)
    nt = seq // tt
    rows = batch * seq
    hb = tt // 8

    def row_spec(width):
        return pl.BlockSpec((tt, width), lambda b, i: (b * nt + i, 0))

    def halo_spec(width):
        return pl.BlockSpec((8, width), lambda b, i: (jnp.maximum((b * nt + i) * hb - 1, 0), 0))

    def const_spec(shape):
        return pl.BlockSpec(shape, lambda b, i: (0,) * len(shape))

    out_sds = jax.ShapeDtypeStruct((rows, RW_WIDTH), F32)
    return pl.pallas_call(
        _rw_prep_body,
        grid=(batch, nt),
        in_specs=[row_spec(3 * RW_WIDTH), row_spec(RW_LORA_PAD),
                  halo_spec(3 * RW_WIDTH), halo_spec(RW_LORA_PAD),
                  pl.BlockSpec((None, 1, RW_PAD), lambda b, i: (b, 0, 0)),
                  const_spec((1, RW_PAD)),
                  const_spec((1, RW_WIDTH)), const_spec((1, RW_WIDTH)), const_spec((1, RW_WIDTH)),
                  const_spec((1, RW_WIDTH)), const_spec((1, RW_WIDTH)),
                  const_spec((128, RW_WIDTH)), const_spec((128, RW_WIDTH)),
                  const_spec((128, RW_WIDTH)), const_spec((128, RW_WIDTH)),
                  const_spec((256, RW_WIDTH)), const_spec((256, RW_WIDTH)),
                  const_spec((RW_WIDTH, RW_WIDTH))],
        out_specs=[row_spec(RW_WIDTH)] * 8,
        out_shape=[out_sds] * 8,
        compiler_params=_params("parallel", "arbitrary"),
        name="rw_prep",
    )(z_main, z_lora, z_main, z_lora, z_prev.reshape(batch, 1, RW_PAD), p["mu"], p["w0"], p["a0"], p["k_k"],
      p["k_a"], p["r_k"],
      p["w2h"], p["w2l"], p["a2h"], p["a2l"], p["g2h"], p["g2l"], p["ones_bd"])


def _rw_scan_body(kk_ref, w_ref, b_ref, km_ref, r_ref, v_ref, s0_ref, o_ref, s_ref, ops_ref, *, nv, tc):
    @pl.when(pl.program_id(0) == 0)
    def _():
        s_ref[...] = s0_ref[...]

    kdim = s_ref.shape[1]

    def step(t, _):
        th = lax.shift_right_logical(t, 3)
        tl = lax.bitwise_and(t, TIME_SUB - 1)
        for n, ref in enumerate((kk_ref, w_ref, b_ref, km_ref, r_ref)):
            ops_ref[n] = ref[th, pl.ds(tl, kdim, stride=TIME_SUB), :]
        kk, w, bm, km, r = (ops_ref[n] for n in range(5))
        for vp in range(nv):
            s = s_ref[vp]
            sa = -jnp.sum(s * kk, axis=0, keepdims=True)
            s = s * w + sa * bm + v_ref[th, pl.ds(vp * TIME_SUB + tl, 1), :] * km
            s_ref[vp] = s
            o_ref[th, pl.ds(vp * TIME_SUB + tl, 1), :] = jnp.sum(s * r, axis=0, keepdims=True)
        return 0

    lax.fori_loop(0, tc, step, 0)


def rw_scan(kk, w, bm, km, r, v, s0):
    groups, rows, lanes = kk.shape
    kdim = rows // TIME_SUB
    nv = v.shape[1] // TIME_SUB
    t_len = groups * TIME_SUB
    tc = min(t_len, 16)
    gc = tc // TIME_SUB
    op_spec = pl.BlockSpec((gc, rows, lanes), lambda i: (i, 0, 0))
    v_spec = pl.BlockSpec((gc, nv * TIME_SUB, lanes), lambda i: (i, 0, 0))
    s_spec = pl.BlockSpec((nv, kdim, lanes), lambda i: (0, 0, 0))
    return pl.pallas_call(
        functools.partial(_rw_scan_body, nv=nv, tc=tc),
        grid=(t_len // tc,),
        in_specs=[op_spec] * 5 + [v_spec, s_spec],
        out_specs=[v_spec, s_spec],
        out_shape=[jax.ShapeDtypeStruct((groups, nv * TIME_SUB, lanes), F32),
                   jax.ShapeDtypeStruct((nv, kdim, lanes), F32)],
        scratch_shapes=[pltpu.VMEM((5, kdim, lanes), F32)],
        compiler_params=_params("arbitrary"),
        name="rw_scan",
    )(kk, w, bm, km, r, v, s0)


def _to_scan_body(x_ref, o_ref, xt_ref, *, value_rows):
    nb, tt = x_ref.shape[:2]
    nbh = nb * RW_HEADS
    for b in range(nb):
        xt_ref[b * RW_WIDTH:(b + 1) * RW_WIDTH, :] = x_ref[b].T

    def put(row, a):
        o_ref[:, row * TIME_SUB:(row + 1) * TIME_SUB, :] = a.T.reshape(tt // TIME_SUB, TIME_SUB, 128)

    if value_rows:
        for vp in range(RW_HEAD_DIM // 2):
            a0 = xt_ref[pl.ds(2 * vp, nbh, stride=RW_HEAD_DIM), :]
            a1 = xt_ref[pl.ds(2 * vp + 1, nbh, stride=RW_HEAD_DIM), :]
            put(vp, jnp.concatenate([a0, a1], axis=0))
    else:
        for c in range(RW_HEAD_DIM):
            a = xt_ref[pl.ds(c, nbh, stride=RW_HEAD_DIM), :]
            put(c, jnp.concatenate([a, a], axis=0))


def to_scan(x, batch, seq, value_rows):
    tt = 128
    rows_out = (RW_HEAD_DIM // 2 if value_rows else RW_HEAD_DIM) * TIME_SUB
    return pl.pallas_call(
        functools.partial(_to_scan_body, value_rows=value_rows),
        grid=(seq // tt,),
        in_specs=[pl.BlockSpec((batch, tt, RW_WIDTH), lambda i: (0, i, 0))],
        out_specs=pl.BlockSpec((tt // TIME_SUB, rows_out, 128), lambda i: (i, 0, 0)),
        out_shape=jax.ShapeDtypeStruct((seq // TIME_SUB, rows_out, 128), F32),
        scratch_shapes=[pltpu.VMEM((batch * RW_WIDTH, tt), F32)],
        compiler_params=_params("parallel"),
        name="to_scan",
    )(x.reshape(batch, seq, RW_WIDTH))


def _from_scan_body(o_ref, x_ref, xt_ref):
    nb, tt = x_ref.shape[:2]
    nbh = nb * RW_HEADS
    for vp in range(RW_HEAD_DIM // 2):
        a = o_ref[:, vp * TIME_SUB:(vp + 1) * TIME_SUB, :].reshape(tt, 128).T
        xt_ref[pl.ds(2 * vp, nbh, stride=RW_HEAD_DIM), :] = a[:nbh]
        xt_ref[pl.ds(2 * vp + 1, nbh, stride=RW_HEAD_DIM), :] = a[nbh:]
    for b in range(nb):
        x_ref[b] = xt_ref[b * RW_WIDTH:(b + 1) * RW_WIDTH, :].T


def from_scan(o, batch, seq):
    tt = 128
    out = pl.pallas_call(
        _from_scan_body,
        grid=(seq // tt,),
        in_specs=[pl.BlockSpec((tt // TIME_SUB, RW_HEAD_DIM // 2 * TIME_SUB, 128), lambda i: (i, 0, 0))],
        out_specs=pl.BlockSpec((batch, tt, RW_WIDTH), lambda i: (0, i, 0)),
        out_shape=jax.ShapeDtypeStruct((batch, seq, RW_WIDTH), F32),
        scratch_shapes=[pltpu.VMEM((batch * RW_WIDTH, tt), F32)],
        compiler_params=_params("parallel"),
        name="from_scan",
    )(o)
    return out.reshape(batch * seq, RW_WIDTH)


def _rw_post_body(o_ref, bonus_ref, g_ref, lw_ref, lb_ref, ones_ref, out_ref):
    o = o_ref[...]
    ones_bd = ones_ref[...]
    inv = 1.0 / RW_HEAD_DIM
    mu = _head_sum(o, ones_bd) * inv
    d = o - mu
    var = _head_sum(d * d, ones_bd) * inv
    y = d * lax.rsqrt(var + GN_EPS) * lw_ref[...] + lb_ref[...]
    out_ref[...] = ((y + bonus_ref[...]) * g_ref[...]).astype(out_ref.dtype)


def rw_post(o, bonus, g, lnx_w, lnx_b, ones_bd):
    rows = o.shape[0]
    tt = min(rows, 512)
    row_spec = pl.BlockSpec((tt, RW_WIDTH), lambda i: (i, 0))
    vec_spec = pl.BlockSpec((1, RW_WIDTH), lambda i: (0, 0))
    return pl.pallas_call(
        _rw_post_body,
        grid=(rows // tt,),
        in_specs=[row_spec, row_spec, row_spec, vec_spec, vec_spec,
                  pl.BlockSpec((RW_WIDTH, RW_WIDTH), lambda i: (0, 0))],
        out_specs=row_spec,
        out_shape=jax.ShapeDtypeStruct((rows, RW_WIDTH), BF16),
        compiler_params=_params("parallel"),
        name="rw_post",
    )(o, bonus, g, lnx_w, lnx_b, ones_bd)


def rwkv7(zb, z_prev, wkv0, p, batch, seq):
    r, w, km, kk, bm, v, g, bonus = rw_prep(zb[0], zb[1], z_prev, p, batch, seq)
    bh = batch * RW_HEADS
    vpar = 128 // bh
    nv = RW_HEAD_DIM // vpar
    s0 = wkv0.reshape(batch, RW_HEADS, nv, vpar, RW_HEAD_DIM).transpose(2, 4, 3, 0, 1).reshape(
        nv, RW_HEAD_DIM, 128)
    if vpar == 2 and seq % 128 == 0:
        ops = [to_scan(x, batch, seq, False) for x in (kk, w, bm, km, r)]
        o_l, s_l = rw_scan(*ops, to_scan(v, batch, seq, True), s0)
        o = from_scan(o_l, batch, seq)
    else:
        tg = seq // TIME_SUB

        def key_layout(x):
            x = x.reshape(batch, tg, TIME_SUB, RW_HEADS, RW_HEAD_DIM).transpose(1, 4, 2, 0, 3)
            return jnp.concatenate([x.reshape(tg, RW_HEAD_DIM * TIME_SUB, bh)] * vpar, axis=-1)

        v_l = v.reshape(batch, tg, TIME_SUB, RW_HEADS, nv, vpar).transpose(1, 4, 2, 5, 0, 3).reshape(
            tg, nv * TIME_SUB, 128)
        o_l, s_l = rw_scan(key_layout(kk), key_layout(w), key_layout(bm), key_layout(km), key_layout(r),
                           v_l, s0)
        o = o_l.reshape(tg, nv, TIME_SUB, vpar, batch, RW_HEADS).transpose(4, 0, 2, 5, 1, 3).reshape(
            batch * seq, RW_WIDTH)
    s_fin = s_l.reshape(nv, RW_HEAD_DIM, vpar, batch, RW_HEADS).transpose(3, 4, 0, 2, 1).reshape(
        batch, RW_HEADS, RW_HEAD_DIM, RW_HEAD_DIM)
    out = rw_post(o, bonus, g, p["lnx_w"], p["lnx_b"], p["ones_bd"])
    return out, s_fin


def _gmlp_body(u_ref, v_ref, lnw_ref, lnb_ref, ws_ref, bs_ref, o_ref, *vn_refs, inner):
    v = v_ref[...].astype(F32)
    mu = jnp.mean(v, axis=-1, keepdims=True)
    var = jnp.mean(jnp.square(v - mu), axis=-1, keepdims=True)
    vn = (v - mu) * lax.rsqrt(var + LN_EPS) * lnw_ref[...] + lnb_ref[...]
    if vn_refs:
        vn_refs[0][...] = vn
    n = v.shape[0]
    row = lax.broadcasted_iota(jnp.int32, (n, n), 0)
    col = lax.broadcasted_iota(jnp.int32, (n, n), 1)
    causal = (col <= row) & (col >= row - row % inner)
    for gi in range(C_GROUPS):
        sl = slice(gi * C_GROUP_DIM, (gi + 1) * C_GROUP_DIM)
        wg = jnp.where(causal, ws_ref[gi], 0.0).astype(BF16)
        mixed = jnp.dot(wg, vn[:, sl].astype(BF16), preferred_element_type=F32) + bs_ref[:, gi:gi + 1]
        o_ref[:, sl] = (u_ref[:, sl].astype(F32) * mixed).astype(o_ref.dtype)


def gmlp_gate(z, ln_w, ln_b, ws, bs_t, rows_per_step, inner, want_v):
    rows = z.shape[0]
    n = rows_per_step
    out_shape = [jax.ShapeDtypeStruct((rows, C_WIDTH), BF16)]
    out_specs = [pl.BlockSpec((n, C_WIDTH), lambda i: (i, 0))]
    if want_v:
        out_shape.append(jax.ShapeDtypeStruct((rows, C_WIDTH), F32))
        out_specs.append(pl.BlockSpec((n, C_WIDTH), lambda i: (i, 0)))
    res = pl.pallas_call(
        functools.partial(_gmlp_body, inner=inner),
        grid=(rows // n,),
        in_specs=[pl.BlockSpec((n, C_WIDTH), lambda i: (i, 0)),
                  pl.BlockSpec((n, C_WIDTH), lambda i: (i, 1)),
                  pl.BlockSpec((1, C_WIDTH), lambda i: (0, 0)),
                  pl.BlockSpec((1, C_WIDTH), lambda i: (0, 0)),
                  pl.BlockSpec((C_GROUPS, n, n), lambda i: (0, 0, 0)),
                  pl.BlockSpec((n, C_GROUPS), lambda i: (0, 0))],
        out_specs=out_specs,
        out_shape=out_shape,
        compiler_params=_params("parallel"),
        name="gmlp_gate",
    )(z, z, ln_w, ln_b, ws, bs_t)
    return res if want_v else (res[0], None)


def _conv_gate_body(ug_ref, uv_ref, hg_ref, hv_ref, pg_ref, pv_ref, wg_ref, wv_ref, bg_ref, bv_ref, o_ref):
    i = pl.program_id(1)

    def conv(up_ref, halo_ref, prev_ref, w_ref, b_ref):
        h2 = jnp.where(i == 0, prev_ref[...], halo_ref[6:8, :])
        return _conv3(up_ref[...], h2, w_ref, b_ref)

    gate = conv(ug_ref, hg_ref, pg_ref, wg_ref, bg_ref)
    val = conv(uv_ref, hv_ref, pv_ref, wv_ref, bv_ref)
    o_ref[...] = (jax.nn.silu(gate) * val).astype(o_ref.dtype)


def conv_gate(up, prev, conv_w, conv_b, batch, seq):
    tt = min(seq, 256)
    nt = seq // tt
    hb = tt // 8
    tn = 1024
    nj = FFN_DIM // tn

    def halo_map(off):
        return lambda b, i, j: (jnp.maximum((b * nt + i) * hb - 1, 0), j + off)

    def col_spec(shape, off):
        return pl.BlockSpec(shape, lambda b, i, j: (0, j + off))

    return pl.pallas_call(
        _conv_gate_body,
        grid=(batch, nt, nj),
        in_specs=[pl.BlockSpec((tt, tn), lambda b, i, j: (b * nt + i, j)),
                  pl.BlockSpec((tt, tn), lambda b, i, j: (b * nt + i, j + nj)),
                  pl.BlockSpec((8, tn), halo_map(0)),
                  pl.BlockSpec((8, tn), halo_map(nj)),
                  pl.BlockSpec((None, 2, tn), lambda b, i, j: (b, 0, j)),
                  pl.BlockSpec((None, 2, tn), lambda b, i, j: (b, 0, j + nj)),
                  col_spec((CONV_W, tn), 0), col_spec((CONV_W, tn), nj),
                  col_spec((1, tn), 0), col_spec((1, tn), nj)],
        out_specs=pl.BlockSpec((tt, tn), lambda b, i, j: (b * nt + i, j)),
        out_shape=jax.ShapeDtypeStruct((batch * seq, FFN_DIM), BF16),
        compiler_params=_params("parallel", "parallel", "parallel"),
        name="conv_gate",
    )(up, up, up, up, prev, prev, conv_w, conv_w, conv_b, conv_b)


def _conv3(up, h2, w_ref, b_ref):
    row = lax.broadcasted_iota(jnp.int32, up.shape, 0)
    x1 = jnp.where(row == 0, h2[1:2, :], pltpu.roll(up, 1, axis=0))
    x2 = jnp.where(row == 0, h2[0:1, :], jnp.where(row == 1, h2[1:2, :], pltpu.roll(up, 2, axis=0)))
    return b_ref[...] + (x2 * w_ref[0:1, :] + x1 * w_ref[1:2, :] + up * w_ref[2:3, :])


def _up_conv_body(x_ref, xh_ref, wg_ref, wv_ref, pg_ref, pv_ref, cwg_ref, cwv_ref, cbg_ref, cbv_ref,
                  o_ref, tg_ref, tv_ref, wgb_ref, wvb_ref, *, tiles_per_batch):
    i = pl.program_id(1)
    first = i % tiles_per_batch == 0
    x = x_ref[...]
    xh = xh_ref[...]
    tm = x.shape[0]

    @pl.when(i == 0)
    def _():
        wgb_ref[...] = wg_ref[...].astype(BF16)
        wvb_ref[...] = wv_ref[...].astype(BF16)

    def half(w_ref, prev_ref, cw_ref, cb_ref, tail_ref):
        w = w_ref[...]
        up = jnp.dot(x, w, preferred_element_type=F32)
        up_before = jnp.dot(xh, w, preferred_element_type=F32)
        h2 = jnp.where(first, prev_ref[...], up_before[14:16, :])
        tail_ref[...] = up[tm - 8:, :]
        return _conv3(up, h2, cw_ref, cb_ref)

    gate = half(wgb_ref, pg_ref, cwg_ref, cbg_ref, tg_ref)
    val = half(wvb_ref, pv_ref, cwv_ref, cbv_ref, tv_ref)
    o_ref[...] = (jax.nn.silu(gate) * val).astype(o_ref.dtype)


def ffn_up_conv(x, w_up, layer, prev, conv_w, conv_b, batch, seq):
    assert seq % 1024 == 0
    m, k = x.shape
    tm, tn = 1024, 512
    tiles_per_batch = seq // tm
    nj = FFN_DIM // tn
    hb = tm // 16

    def col_spec(shape, off):
        return pl.BlockSpec(shape, lambda j, i: (0, j + off))

    def w_spec(off):
        return pl.BlockSpec((None, k, tn), lambda j, i: (layer, 0, j + off))

    def batch_spec(rows, off):
        return pl.BlockSpec((None, rows, tn), lambda j, i: (i // tiles_per_batch, 0, j + off))

    hid, tail_g, tail_v = pl.pallas_call(
        functools.partial(_up_conv_body, tiles_per_batch=tiles_per_batch),
        grid=(nj, m // tm),
        in_specs=[pl.BlockSpec((tm, k), lambda j, i: (i, 0)),
                  pl.BlockSpec((16, k), lambda j, i: (jnp.maximum(i * hb - 1, 0), 0)),
                  w_spec(0), w_spec(nj),
                  batch_spec(2, 0), batch_spec(2, nj),
                  col_spec((CONV_W, tn), 0), col_spec((CONV_W, tn), nj),
                  col_spec((1, tn), 0), col_spec((1, tn), nj)],
        out_specs=[pl.BlockSpec((tm, tn), lambda j, i: (i, j)),
                   pl.BlockSpec((None, 8, tn), lambda j, i: (i, 0, j)),
                   pl.BlockSpec((None, 8, tn), lambda j, i: (i, 0, j))],
        out_shape=[jax.ShapeDtypeStruct((m, FFN_DIM), BF16),
                   jax.ShapeDtypeStruct((m // tm, 8, FFN_DIM), F32),
                   jax.ShapeDtypeStruct((m // tm, 8, FFN_DIM), F32)],
        scratch_shapes=[pltpu.VMEM((k, tn), BF16), pltpu.VMEM((k, tn), BF16)],
        compiler_params=_params("parallel", "arbitrary"),
        name="ffn_up_conv",
    )(x, x, w_up, w_up, prev, prev, conv_w, conv_w, conv_b, conv_b)
    tails = jnp.concatenate([tail_g, tail_v], axis=-1)
    return hid, tails.reshape(batch, tiles_per_batch, 8, 2 * FFN_DIM)[:, -1]


_LORA_SEGS = ((LORA_W, 128), (LORA_A, 128), (LORA_G, 256))


def _pad_segs(x):
    parts, start = [], 0
    for width, padded in _LORA_SEGS:
        seg = x[..., start:start + width]
        parts.append(jnp.pad(seg, [(0, 0)] * (x.ndim - 1) + [(0, padded - width)]))
        start += width
    return jnp.concatenate(parts, axis=-1)


def _pad_cols(x, segs=None):
    c = 3 * RW_WIDTH
    return jnp.concatenate([x[..., :c], _pad_segs(x[..., c:])], axis=-1)


def _unpad_cols(x):
    c = 3 * RW_WIDTH
    return jnp.concatenate([x[..., :c], x[..., c:c + LORA_W], x[..., c + 128:c + 128 + LORA_A],
                            x[..., c + 256:c + 256 + LORA_G]], axis=-1)


def _pad_rows(w, padded):
    return jnp.pad(w, ((0, padded - w.shape[0]), (0, 0)))


def kernel(x_prompt, x_sample, cache_k, cache_v, page_table, state_shift, state_wkv, state_conv, norm_mix, norm_ffn, norm_final, w_in_even, sb_bias, mu_shift, w0, w2, a0, a2, g2, k_k, k_a, r_k, lnx_w, lnx_b, w_out_even, w_in_odd, ln_v_w, ln_v_b, w_spatial, b_spatial, w_out_odd, w_up, conv_w, conv_b, w_down):
    bp, sp = x_prompt.shape[:2]
    bs, ts = x_sample.shape[:2]
    xp = x_prompt.reshape(bp * sp, D_MODEL)
    xs = x_sample.reshape(bs * ts, D_MODEL)
    groups = ((bp, sp), (bs, ts))

    head_id = jnp.arange(RW_WIDTH) // RW_HEAD_DIM
    ones_bd = (head_id[:, None] == head_id[None, :]).astype(BF16)

    k_out, v_out, sh_out, wkv_out, cv_out, chunkv_out = ([], []), ([], []), ([], []), ([], []), ([], []), []
    xs_all = [xp, xs]
    for l in range(DEPTH):
        i = l // 2
        h_all = [rmsnorm(x, norm_mix[l], BF16) for x in xs_all]
        if l % 2 == 0:
            q_all, k_all, v_all = (matmul(h_all[0], w_in_even, layer=i, col0=c * SB_WIDTH, n=SB_WIDTH,
                                          rider=h_all[1]) for c in range(3))
            zm_all = matmul(h_all[0], w_in_even, layer=i, col0=3 * SB_WIDTH, n=3 * RW_WIDTH, rider=h_all[1])
            w_lora = _pad_segs(w_in_even[i][:, 3 * SB_WIDTH + 3 * RW_WIDTH:])
            zl_all = matmul(h_all[0], w_lora, rider=h_all[1])
            mixed = []
            w2h, w2l = _split_bf16(_pad_rows(w2[i], 128))
            a2h, a2l = _split_bf16(_pad_rows(a2[i], 128))
            g2h, g2l = _split_bf16(_pad_rows(g2[i], 256))
            p = dict(mu=_pad_cols(mu_shift[i][None, :], _LORA_SEGS), w0=w0[i][None, :], a0=a0[i][None, :],
                     k_k=k_k[i][None, :], k_a=k_a[i][None, :], r_k=r_k[i].reshape(1, RW_WIDTH),
                     w2h=w2h, w2l=w2l, a2h=a2h, a2l=a2l, g2h=g2h, g2l=g2l, ones_bd=ones_bd,
                     lnx_w=lnx_w[i][None, :], lnx_b=lnx_b[i][None, :])
            for gi, (nb, nt) in enumerate(groups):
                q_rows, k_rows, v_rows, zb = q_all[gi], k_all[gi], v_all[gi], (zm_all[gi], zl_all[gi])
                if gi == 0:
                    att = sb_prompt(q_rows, k_rows, v_rows, sb_bias[i], nb, nt)
                    z_prev = jnp.zeros((nb, RW_PAD), F32)
                    wkv0 = jnp.zeros((nb, RW_HEADS, RW_HEAD_DIM, RW_HEAD_DIM), F32)
                else:
                    att = sb_sample(q_rows, k_rows, v_rows, sb_bias[i], cache_k, cache_v, i, page_table)
                    z_prev = _pad_cols(state_shift[i], _LORA_SEGS)
                    wkv0 = state_wkv[i]
                rw, s_fin = rwkv7(zb, z_prev, wkv0, p, nb, nt)
                mixed.append([att, rw])
                k_out[gi].append(k_rows.reshape(nb, nt, SB_HEADS, SB_HEAD_DIM))
                v_out[gi].append(v_rows.reshape(nb, nt, SB_HEADS, SB_HEAD_DIM))
                last_row = jnp.concatenate([zb[0].reshape(nb, nt, -1)[:, -1], zb[1].reshape(nb, nt, -1)[:, -1]],
                                           axis=-1)
                sh_out[gi].append(_unpad_cols(last_row))
                wkv_out[gi].append(s_fin)
            xs_all = list(matmul(mixed[0], w_out_even, layer=i, res=xs_all[0], rider=mixed[1],
                                 rider_res=xs_all[1]))
        else:
            lnw = ln_v_w[i][None, :]
            lnb = ln_v_b[i][None, :]
            z_all = matmul(h_all[0], w_in_odd, layer=i, act="gelu", rider=h_all[1], out_dtype=BF16)
            mixed = []
            for gi, (nb, nt) in enumerate(groups):
                z = z_all[gi]
                if gi == 0:
                    gated, _ = gmlp_gate(z, lnw, lnb, w_spatial[i], b_spatial[i].T, CHUNK, CHUNK, False)
                else:
                    eye = jnp.eye(nb, dtype=F32)
                    ws_s = w_spatial[i][:, :nt, :nt]
                    ws_bd = (eye[None, :, None, :, None] * ws_s[:, None, :, None, :]).reshape(
                        C_GROUPS, nb * nt, nb * nt)
                    bs_t = jnp.tile(b_spatial[i][:, :nt].T, (nb, 1))
                    gated, v_rows = gmlp_gate(z, lnw, lnb, ws_bd, bs_t, nb * nt, nt, True)
                    chunkv_out.append(v_rows.reshape(nb, nt, C_WIDTH))
                mixed.append(gated)
            xs_all = list(matmul(mixed[0], w_out_odd, layer=i, res=xs_all[0], rider=mixed[1],
                                 rider_res=xs_all[1]))
        hidden = []
        for gi, (nb, nt) in enumerate(groups):
            hf = rmsnorm(xs_all[gi], norm_ffn[l], BF16)
            if gi == 0:
                prev = jnp.zeros((nb, CONV_W - 1, 2 * FFN_DIM), F32)
                hid, up_tail = ffn_up_conv(hf, w_up, l, prev, conv_w[l], conv_b[l][None, :], nb, nt)
            else:
                prev = state_conv[l]
                up = matmul(hf, w_up, layer=l)
                hid = conv_gate(up, prev, conv_w[l], conv_b[l][None, :], nb, nt)
                up_tail = up.reshape(nb, nt, 2 * FFN_DIM)
            hidden.append(hid)
            cv_out[gi].append(jnp.concatenate([prev, up_tail], axis=1)[:, -(CONV_W - 1):])
        xs_all = list(matmul(hidden[0], w_down, layer=l, res=xs_all[0], rider=hidden[1], rider_res=xs_all[1]))
    y_prompt = rmsnorm(xs_all[0], norm_final, F32).reshape(bp, sp, D_MODEL)
    y_sample = rmsnorm(xs_all[1], norm_final, F32).reshape(bs, ts, D_MODEL)
    st = jnp.stack
    return (y_prompt, y_sample, st(k_out[0]), st(v_out[0]), st(k_out[1]), st(v_out[1]),
            st(sh_out[0]), st(sh_out[1]), st(wkv_out[0]), st(wkv_out[1]),
            st(cv_out[0]), st(cv_out[1]), st(chunkv_out))
```

```python
import functools

import jax
import jax.numpy as jnp
from jax import lax
from jax.experimental import pallas as pl
from jax.experimental.pallas import tpu as pltpu

F32 = jnp.float32
BF16 = jnp.bfloat16

D_MODEL = 2048
DEPTH = 4
PAGE_SIZE = 128
SB_HEAD_DIM = 128
SB_HEADS = 8
SB_WIDTH = 1024
RW_HEAD_DIM = 64
RW_WIDTH = 1024
RW_HEADS = 16
LORA_W = 64
LORA_A = 64
LORA_G = 160
RW_IN = 3 * RW_WIDTH + LORA_W + LORA_A + LORA_G
RW_LORA_PAD = 128 + 128 + 256
RW_PAD = 3 * RW_WIDTH + RW_LORA_PAD
GN_EPS = 64e-5
C_WIDTH = 2 * D_MODEL
C_GROUPS = 8
C_GROUP_DIM = C_WIDTH // C_GROUPS
CHUNK = 128
FFN_DIM = 2 * D_MODEL
CONV_W = 3
RMS_EPS = 1e-6
LN_EPS = 1e-5
LOG2E = 1.4426950408889634
SB_SCALE2 = SB_HEAD_DIM ** -0.5 * LOG2E
TIME_SUB = 8

VMEM_LIMIT_BYTES = 52 * 1024 * 1024


def _params(*sem):
    return pltpu.CompilerParams(dimension_semantics=sem, vmem_limit_bytes=VMEM_LIMIT_BYTES)


def _rmsnorm_body(x_ref, g_ref, o_ref):
    x = x_ref[...]
    ms = jnp.mean(x * x, axis=-1, keepdims=True)
    o_ref[...] = (x * lax.rsqrt(ms + RMS_EPS) * g_ref[...]).astype(o_ref.dtype)


def rmsnorm(x, g, out_dtype):
    m, d = x.shape
    tm = min(m, 512)
    return pl.pallas_call(
        _rmsnorm_body,
        grid=(m // tm,),
        in_specs=[pl.BlockSpec((tm, d), lambda i: (i, 0)),
                  pl.BlockSpec((1, d), lambda i: (0, 0))],
        out_specs=pl.BlockSpec((tm, d), lambda i: (i, 0)),
        out_shape=jax.ShapeDtypeStruct((m, d), out_dtype),
        compiler_params=_params("parallel"),
        name="rmsnorm",
    )(x, g.reshape(1, d))


def _matmul_body(*refs, act, n_parts, has_res, has_rider):
    refs = list(refs)
    wb_ref = refs.pop()
    groups = 2 if has_rider else 1
    x_refs = [refs[g * n_parts:(g + 1) * n_parts] for g in range(groups)]
    w_ref = refs[groups * n_parts]
    r_refs = refs[groups * n_parts + 1:][:groups] if has_res else [None] * groups
    o_refs = refs[-groups:]

    def product(g):
        acc, k0 = None, 0
        for x_ref in x_refs[g]:
            kp = x_ref.shape[1]
            part = jnp.dot(x_ref[...], wb_ref[k0:k0 + kp, :], preferred_element_type=F32)
            acc = part if acc is None else acc + part
            k0 += kp
        if act == "gelu":
            acc = jax.nn.gelu(acc)
        if has_res:
            acc = r_refs[g][...] + acc
        o_refs[g][...] = acc.astype(o_refs[g].dtype)

    @pl.when(pl.program_id(1) == 0)
    def _():
        wb_ref[...] = w_ref[...].astype(BF16)
        if has_rider:
            product(1)

    product(0)


def matmul(x, w, layer=None, col0=0, n=None, res=None, act=None, rider=None, rider_res=None, out_dtype=F32):
    xs = list(x) if isinstance(x, (list, tuple)) else [x]
    has_res, has_rider = res is not None, rider is not None
    groups = [xs]
    if has_rider:
        groups.append(list(rider) if isinstance(rider, (list, tuple)) else [rider])
    m = xs[0].shape[0]
    k = sum(p.shape[1] for p in xs)
    n = w.shape[-1] - col0 if n is None else n
    tm = min(m, 1024)
    tn = 1024 if (k <= 2048 and n % 1024 == 0 and col0 % 1024 == 0) else 512
    c0 = col0 // tn
    row_tiled = lambda j, i: (i, 0)
    whole = lambda j, i: (0, 0)
    in_specs, args = [], []
    for g, parts in enumerate(groups):
        for p in parts:
            rows = tm if g == 0 else p.shape[0]
            in_specs.append(pl.BlockSpec((rows, p.shape[1]), row_tiled if g == 0 else whole))
            args.append(p)
    if layer is None:
        in_specs.append(pl.BlockSpec((k, tn), lambda j, i: (0, j + c0)))
    else:
        in_specs.append(pl.BlockSpec((None, k, tn), lambda j, i: (layer, 0, j + c0)))
    args.append(w)
    out_specs = [pl.BlockSpec((tm, tn), lambda j, i: (i, j))]
    out_shape = [jax.ShapeDtypeStruct((m, n), out_dtype)]
    if has_res:
        in_specs.append(pl.BlockSpec((tm, tn), lambda j, i: (i, j)))
        args.append(res)
    if has_rider:
        mr = groups[1][0].shape[0]
        out_specs.append(pl.BlockSpec((mr, tn), lambda j, i: (0, j)))
        out_shape.append(jax.ShapeDtypeStruct((mr, n), F32))
        if has_res:
            in_specs.append(pl.BlockSpec((mr, tn), lambda j, i: (0, j)))
            args.append(rider_res)
    out = pl.pallas_call(
        functools.partial(_matmul_body, act=act, n_parts=len(xs), has_res=has_res, has_rider=has_rider),
        grid=(n // tn, m // tm),
        in_specs=in_specs,
        out_specs=out_specs,
        out_shape=out_shape,
        scratch_shapes=[pltpu.VMEM((k, tn), BF16)],
        compiler_params=_params("parallel", "arbitrary"),
        name="matmul",
    )(*args)
    return tuple(out) if has_rider else out[0]


def _tri_and_mask(n):
    row = lax.broadcasted_iota(jnp.int32, (n, n), 0)
    col = lax.broadcasted_iota(jnp.int32, (n, n), 1)
    tri = jnp.where(row > col, 1.0, 0.0).astype(BF16)
    return tri, row, col


SB_SUB = 128


def _sb_block(q, kb, vb, bias, carry, acc, tri, mask):
    z2 = lax.dot_general(q, kb, (((1,), (1,)), ((), ())), preferred_element_type=F32) + bias
    sp = jnp.log2(1.0 + jnp.exp2(jnp.minimum(z2, -z2)))
    stay_all = jnp.maximum(z2, 0.0) + sp
    stay = stay_all if mask is None else jnp.where(mask, stay_all, 0.0)
    hi = stay.astype(BF16)
    lo = (stay - hi.astype(F32)).astype(BF16)
    parts = [None] * (z2.shape[1] // SB_SUB)
    for s in reversed(range(len(parts))):
        sl = slice(s * SB_SUB, (s + 1) * SB_SUB)
        parts[s] = (carry + jnp.dot(hi[:, sl], tri, preferred_element_type=F32)
                    + jnp.dot(lo[:, sl], tri, preferred_element_type=F32))
        carry = carry + jnp.sum(stay[:, sl], axis=1, keepdims=True)
    later = parts[0] if len(parts) == 1 else jnp.concatenate(parts, axis=1)
    att = jnp.exp2(z2 - stay_all - later)
    if mask is not None:
        att = jnp.where(mask, att, 0.0)
    acc = acc + jnp.dot(att.astype(BF16), vb, preferred_element_type=F32)
    return carry, acc


def _sb_prompt_body(bias_ref, q_ref, k_ref, v_ref, o_ref, *, tq):
    h = pl.program_id(1)
    i = pl.program_id(2)
    bias = bias_ref[h] * LOG2E
    q = (q_ref[...] * SB_SCALE2).astype(BF16)
    tri, _, _ = _tri_and_mask(SB_SUB)

    def blk(j, rows, width, carry, acc, mask):
        off = pl.multiple_of(j * tq, tq)
        kb = k_ref[pl.ds(off, width), :].astype(BF16)
        vb = v_ref[pl.ds(off, width), :].astype(BF16)
        return _sb_block(q[rows], kb, vb, bias, carry, acc, tri, mask)

    row = lax.broadcasted_iota(jnp.int32, (tq, tq), 0)
    col = lax.broadcasted_iota(jnp.int32, (tq, tq), 1)
    carry, acc = blk(i, slice(None), tq, jnp.zeros((tq, 1), F32), jnp.zeros((tq, SB_HEAD_DIM), F32),
                     col < row)

    def body(n, c):
        return blk(i - 1 - n, slice(None), tq, c[0], c[1], None)

    carry, acc = lax.fori_loop(0, i, body, (carry, acc))
    o_ref[...] = acc.astype(o_ref.dtype)


def sb_prompt(q, k, v, bias, batch, seq):
    tq = 512
    nq = seq // tq
    seq_spec = pl.BlockSpec((seq, SB_HEAD_DIM), lambda b, h, i: (b, h))
    return pl.pallas_call(
        functools.partial(_sb_prompt_body, tq=tq),
        grid=(batch, SB_HEADS, nq),
        in_specs=[pl.BlockSpec(memory_space=pltpu.SMEM),
                  pl.BlockSpec((tq, SB_HEAD_DIM), lambda b, h, i: (b * nq + i, h)),
                  seq_spec, seq_spec],
        out_specs=pl.BlockSpec((tq, SB_HEAD_DIM), lambda b, h, i: (b * nq + i, h)),
        out_shape=jax.ShapeDtypeStruct((batch * seq, SB_WIDTH), BF16),
        compiler_params=_params("parallel", "parallel", "arbitrary"),
        name="sb_prompt",
    )(bias, q, k, v)


SB_PAGES_PER_STEP = 16


def _sb_sample_body(pt_ref, qbd_ref, bias_ref, kn_ref, vn_ref, *rest, t_new):
    npg = SB_PAGES_PER_STEP
    k_refs, v_refs = rest[:npg], rest[npg:2 * npg]
    o_ref, carry_ref, acc_ref = rest[2 * npg:]
    j = pl.program_id(1)
    rows = qbd_ref.shape[0]
    q = qbd_ref[...]
    bias = bias_ref[...]
    tri, _, _ = _tri_and_mask(SB_SUB)

    @pl.when(j == 0)
    def _():
        row = lax.broadcasted_iota(jnp.int32, (rows, PAGE_SIZE), 0)
        col = lax.broadcasted_iota(jnp.int32, (rows, PAGE_SIZE), 1)
        mask = col < (row % t_new)
        carry, acc = _sb_block(q, kn_ref[...].astype(BF16), vn_ref[...].astype(BF16), bias[:, :PAGE_SIZE],
                               jnp.zeros((rows, 1), F32), jnp.zeros(acc_ref.shape, F32), tri, mask)
        carry_ref[...] = carry
        acc_ref[...] = acc

    def pages(refs):
        return jnp.concatenate(
            [jnp.concatenate([r[pl.ds(h, PAGE_SIZE, stride=SB_HEADS), :].astype(BF16)
                              for h in range(SB_HEADS)], axis=1) for r in refs], axis=0)

    @pl.when(j > 0)
    def _():
        carry, acc = _sb_block(q, pages(k_refs), pages(v_refs), bias,
                               carry_ref[...], acc_ref[...], tri, None)
        carry_ref[...] = carry
        acc_ref[...] = acc

    @pl.when(j == pl.num_programs(1) - 1)
    def _():
        o_ref[...] = acc_ref[...]


def sb_sample(q, k_new, v_new, bias, cache_k, cache_v, layer, page_table):
    bd, n_pages = page_table.shape
    t_new = q.shape[0] // bd
    rows = SB_HEADS * t_new
    n_pool = cache_k.shape[1]
    npg = SB_PAGES_PER_STEP
    q4 = (q * SB_SCALE2).reshape(bd, t_new, SB_HEADS, SB_HEAD_DIM).transpose(0, 2, 1, 3)
    eye = jnp.eye(SB_HEADS, dtype=F32)
    qbd = (q4[:, :, :, None, :] * eye[None, :, None, :, None]).reshape(bd, rows, SB_WIDTH).astype(BF16)
    bias_rows = jnp.broadcast_to(jnp.repeat(bias * LOG2E, t_new)[:, None], (rows, npg * PAGE_SIZE))
    pad = ((0, 0), (0, PAGE_SIZE - t_new), (0, 0))
    kn = jnp.pad(k_new.reshape(bd, t_new, SB_WIDTH), pad)
    vn = jnp.pad(v_new.reshape(bd, t_new, SB_WIDTH), pad)
    page_rows = PAGE_SIZE * SB_HEADS
    kc = cache_k.reshape(cache_k.shape[0], n_pool, page_rows, SB_HEAD_DIM)
    vc = cache_v.reshape(cache_v.shape[0], n_pool, page_rows, SB_HEAD_DIM)

    def page_spec(m):
        def page_map(b, j, pt):
            return (layer, pt[(b + 1) * n_pages - npg * jnp.maximum(j, 1) + m], 0, 0)
        return pl.BlockSpec((None, None, page_rows, SB_HEAD_DIM), page_map)

    page_specs = [page_spec(m) for m in range(npg)]
    out = pl.pallas_call(
        functools.partial(_sb_sample_body, t_new=t_new),
        grid_spec=pltpu.PrefetchScalarGridSpec(
            num_scalar_prefetch=1,
            grid=(bd, n_pages // npg + 1),
            in_specs=[pl.BlockSpec((None, rows, SB_WIDTH), lambda b, j, pt: (b, 0, 0)),
                      pl.BlockSpec((rows, npg * PAGE_SIZE), lambda b, j, pt: (0, 0)),
                      pl.BlockSpec((None, PAGE_SIZE, SB_WIDTH), lambda b, j, pt: (b, 0, 0)),
                      pl.BlockSpec((None, PAGE_SIZE, SB_WIDTH), lambda b, j, pt: (b, 0, 0))]
            + page_specs + page_specs,
            out_specs=pl.BlockSpec((None, rows, SB_WIDTH), lambda b, j, pt: (b, 0, 0)),
            scratch_shapes=[pltpu.VMEM((rows, 1), F32), pltpu.VMEM((rows, SB_WIDTH), F32)]),
        out_shape=jax.ShapeDtypeStruct((bd, rows, SB_WIDTH), F32),
        compiler_params=_params("parallel", "arbitrary"),
        name="sb_sample",
    )(page_table.reshape(-1), qbd, bias_rows, kn, vn, *([kc] * npg), *([vc] * npg))
    o5 = out.reshape(bd, SB_HEADS, t_new, SB_HEADS, SB_HEAD_DIM)
    idx = jnp.arange(SB_HEADS)
    diag = o5[:, idx, :, idx, :]
    return diag.transpose(1, 2, 0, 3).reshape(bd * t_new, SB_WIDTH).astype(BF16)


def _split_bf16(x):
    hi = x.astype(BF16)
    lo = (x - hi.astype(F32)).astype(BF16)
    return hi, lo


def _dot3(a, b_hi, b_lo):
    a_hi, a_lo = _split_bf16(a)
    return (jnp.dot(a_hi, b_hi, preferred_element_type=F32)
            + jnp.dot(a_hi, b_lo, preferred_element_type=F32)
            + jnp.dot(a_lo, b_hi, preferred_element_type=F32))


def _head_sum(x, ones_bd):
    hi, lo = _split_bf16(x)
    return (jnp.dot(hi, ones_bd, preferred_element_type=F32)
            + jnp.dot(lo, ones_bd, preferred_element_type=F32))


def _rw_prep_body(zm_ref, zl_ref, hm_ref, hl_ref, prev_ref, mu_ref, w0_ref, a0_ref, kk_ref_w, ka_ref, rk_ref,
                  w2h_ref, w2l_ref, a2h_ref, a2l_ref, g2h_ref, g2l_ref, ones_ref,
                  r_out, w_out, km_out, kk_out, b_out, v_out, g_out, bonus_out):
    i = pl.program_id(1)
    zb = jnp.concatenate([zm_ref[...], zl_ref[...]], axis=1)
    halo = jnp.concatenate([hm_ref[7:8, :], hl_ref[7:8, :]], axis=1)
    prev_row = jnp.where(i == 0, prev_ref[...], halo)
    row = lax.broadcasted_iota(jnp.int32, zb.shape, 0)
    shifted = jnp.where(row == 0, prev_row, pltpu.roll(zb, 1, axis=0))
    xz = zb + (shifted - zb) * mu_ref[...]
    c = RW_WIDTH
    r = xz[:, 0:c]
    k = xz[:, c:2 * c]
    v = xz[:, 2 * c:3 * c]
    zw = xz[:, 3 * c:3 * c + 128]
    za = xz[:, 3 * c + 128:3 * c + 256]
    zg = xz[:, 3 * c + 256:3 * c + 512]
    ones_bd = ones_ref[...]
    w_log = -jax.nn.softplus(-(w0_ref[...] + _dot3(jnp.tanh(zw), w2h_ref[...], w2l_ref[...]))) - 0.5
    decay = jnp.exp(-jnp.exp(w_log))
    a = jax.nn.sigmoid(a0_ref[...] + _dot3(za, a2h_ref[...], a2l_ref[...]))
    g = _dot3(jax.nn.sigmoid(zg), g2h_ref[...], g2l_ref[...])
    kk = k * kk_ref_w[...]
    kk = kk / jnp.maximum(jnp.sqrt(_head_sum(kk * kk, ones_bd)), 1e-12)
    k_mod = k * (1.0 + (a - 1.0) * ka_ref[...])
    bonus = _head_sum(r * k_mod * rk_ref[...], ones_bd) * v
    r_out[...] = r
    w_out[...] = decay
    km_out[...] = k_mod
    kk_out[...] = kk
    b_out[...] = kk * a
    v_out[...] = v
    g_out[...] = g
    bonus_out[...] = bonus


def rw_prep(z_main, z_lora, z_prev, p, batch, seq):
    tt = min(seq, 256)
    nt = seq // tt
    rows = batch * seq
    hb = tt // 8

    def row_spec(width):
        return pl.BlockSpec((tt, width), lambda b, i: (b * nt + i, 0))

    def halo_spec(width):
        return pl.BlockSpec((8, width), lambda b, i: (jnp.maximum((b * nt + i) * hb - 1, 0), 0))

    def const_spec(shape):
        return pl.BlockSpec(shape, lambda b, i: (0,) * len(shape))

    out_sds = jax.ShapeDtypeStruct((rows, RW_WIDTH), F32)
    return pl.pallas_call(
        _rw_prep_body,
        grid=(batch, nt),
        in_specs=[row_spec(3 * RW_WIDTH), row_spec(RW_LORA_PAD),
                  halo_spec(3 * RW_WIDTH), halo_spec(RW_LORA_PAD),
                  pl.BlockSpec((None, 1, RW_PAD), lambda b, i: (b, 0, 0)),
                  const_spec((1, RW_PAD)),
                  const_spec((1, RW_WIDTH)), const_spec((1, RW_WIDTH)), const_spec((1, RW_WIDTH)),
                  const_spec((1, RW_WIDTH)), const_spec((1, RW_WIDTH)),
                  const_spec((128, RW_WIDTH)), const_spec((128, RW_WIDTH)),
                  const_spec((128, RW_WIDTH)), const_spec((128, RW_WIDTH)),
                  const_spec((256, RW_WIDTH)), const_spec((256, RW_WIDTH)),
                  const_spec((RW_WIDTH, RW_WIDTH))],
        out_specs=[row_spec(RW_WIDTH)] * 8,
        out_shape=[out_sds] * 8,
        compiler_params=_params("parallel", "arbitrary"),
        name="rw_prep",
    )(z_main, z_lora, z_main, z_lora, z_prev.reshape(batch, 1, RW_PAD), p["mu"], p["w0"], p["a0"], p["k_k"],
      p["k_a"], p["r_k"],
      p["w2h"], p["w2l"], p["a2h"], p["a2l"], p["g2h"], p["g2l"], p["ones_bd"])


def _rw_scan_body(kk_ref, w_ref, b_ref, km_ref, r_ref, v_ref, s0_ref, o_ref, s_ref, ops_ref, *, nv, tc):
    @pl.when(pl.program_id(0) == 0)
    def _():
        s_ref[...] = s0_ref[...]

    kdim = s_ref.shape[1]

    def step(t, _):
        th = lax.shift_right_logical(t, 3)
        tl = lax.bitwise_and(t, TIME_SUB - 1)
        for n, ref in enumerate((kk_ref, w_ref, b_ref, km_ref, r_ref)):
            ops_ref[n] = ref[th, pl.ds(tl, kdim, stride=TIME_SUB), :]
        kk, w, bm, km, r = (ops_ref[n] for n in range(5))
        for vp in range(nv):
            s = s_ref[vp]
            sa = -jnp.sum(s * kk, axis=0, keepdims=True)
            s = s * w + sa * bm + v_ref[th, pl.ds(vp * TIME_SUB + tl, 1), :] * km
            s_ref[vp] = s
            o_ref[th, pl.ds(vp * TIME_SUB + tl, 1), :] = jnp.sum(s * r, axis=0, keepdims=True)
        return 0

    lax.fori_loop(0, tc, step, 0)


def rw_scan(kk, w, bm, km, r, v, s0):
    groups, rows, lanes = kk.shape
    kdim = rows // TIME_SUB
    nv = v.shape[1] // TIME_SUB
    t_len = groups * TIME_SUB
    tc = min(t_len, 16)
    gc = tc // TIME_SUB
    op_spec = pl.BlockSpec((gc, rows, lanes), lambda i: (i, 0, 0))
    v_spec = pl.BlockSpec((gc, nv * TIME_SUB, lanes), lambda i: (i, 0, 0))
    s_spec = pl.BlockSpec((nv, kdim, lanes), lambda i: (0, 0, 0))
    return pl.pallas_call(
        functools.partial(_rw_scan_body, nv=nv, tc=tc),
        grid=(t_len // tc,),
        in_specs=[op_spec] * 5 + [v_spec, s_spec],
        out_specs=[v_spec, s_spec],
        out_shape=[jax.ShapeDtypeStruct((groups, nv * TIME_SUB, lanes), F32),
                   jax.ShapeDtypeStruct((nv, kdim, lanes), F32)],
        scratch_shapes=[pltpu.VMEM((5, kdim, lanes), F32)],
        compiler_params=_params("arbitrary"),
        name="rw_scan",
    )(kk, w, bm, km, r, v, s0)


def _to_scan_body(x_ref, o_ref, xt_ref, *, value_rows):
    nb, tt = x_ref.shape[:2]
    nbh = nb * RW_HEADS
    for b in range(nb):
        xt_ref[b * RW_WIDTH:(b + 1) * RW_WIDTH, :] = x_ref[b].T

    def put(row, a):
        o_ref[:, row * TIME_SUB:(row + 1) * TIME_SUB, :] = a.T.reshape(tt // TIME_SUB, TIME_SUB, 128)

    if value_rows:
        for vp in range(RW_HEAD_DIM // 2):
            a0 = xt_ref[pl.ds(2 * vp, nbh, stride=RW_HEAD_DIM), :]
            a1 = xt_ref[pl.ds(2 * vp + 1, nbh, stride=RW_HEAD_DIM), :]
            put(vp, jnp.concatenate([a0, a1], axis=0))
    else:
        for c in range(RW_HEAD_DIM):
            a = xt_ref[pl.ds(c, nbh, stride=RW_HEAD_DIM), :]
            put(c, jnp.concatenate([a, a], axis=0))


def to_scan(x, batch, seq, value_rows):
    tt = 128
    rows_out = (RW_HEAD_DIM // 2 if value_rows else RW_HEAD_DIM) * TIME_SUB
    return pl.pallas_call(
        functools.partial(_to_scan_body, value_rows=value_rows),
        grid=(seq // tt,),
        in_specs=[pl.BlockSpec((batch, tt, RW_WIDTH), lambda i: (0, i, 0))],
        out_specs=pl.BlockSpec((tt // TIME_SUB, rows_out, 128), lambda i: (i, 0, 0)),
        out_shape=jax.ShapeDtypeStruct((seq // TIME_SUB, rows_out, 128), F32),
        scratch_shapes=[pltpu.VMEM((batch * RW_WIDTH, tt), F32)],
        compiler_params=_params("parallel"),
        name="to_scan",
    )(x.reshape(batch, seq, RW_WIDTH))


def _from_scan_body(o_ref, x_ref, xt_ref):
    nb, tt = x_ref.shape[:2]
    nbh = nb * RW_HEADS
    for vp in range(RW_HEAD_DIM // 2):
        a = o_ref[:, vp * TIME_SUB:(vp + 1) * TIME_SUB, :].reshape(tt, 128).T
        xt_ref[pl.ds(2 * vp, nbh, stride=RW_HEAD_DIM), :] = a[:nbh]
        xt_ref[pl.ds(2 * vp + 1, nbh, stride=RW_HEAD_DIM), :] = a[nbh:]
    for b in range(nb):
        x_ref[b] = xt_ref[b * RW_WIDTH:(b + 1) * RW_WIDTH, :].T


def from_scan(o, batch, seq):
    tt = 128
    out = pl.pallas_call(
        _from_scan_body,
        grid=(seq // tt,),
        in_specs=[pl.BlockSpec((tt // TIME_SUB, RW_HEAD_DIM // 2 * TIME_SUB, 128), lambda i: (i, 0, 0))],
        out_specs=pl.BlockSpec((batch, tt, RW_WIDTH), lambda i: (0, i, 0)),
        out_shape=jax.ShapeDtypeStruct((batch, seq, RW_WIDTH), F32),
        scratch_shapes=[pltpu.VMEM((batch * RW_WIDTH, tt), F32)],
        compiler_params=_params("parallel"),
        name="from_scan",
    )(o)
    return out.reshape(batch * seq, RW_WIDTH)


def _rw_post_body(o_ref, bonus_ref, g_ref, lw_ref, lb_ref, ones_ref, out_ref):
    o = o_ref[...]
    ones_bd = ones_ref[...]
    inv = 1.0 / RW_HEAD_DIM
    mu = _head_sum(o, ones_bd) * inv
    d = o - mu
    var = _head_sum(d * d, ones_bd) * inv
    y = d * lax.rsqrt(var + GN_EPS) * lw_ref[...] + lb_ref[...]
    out_ref[...] = ((y + bonus_ref[...]) * g_ref[...]).astype(out_ref.dtype)


def rw_post(o, bonus, g, lnx_w, lnx_b, ones_bd):
    rows = o.shape[0]
    tt = min(rows, 512)
    row_spec = pl.BlockSpec((tt, RW_WIDTH), lambda i: (i, 0))
    vec_spec = pl.BlockSpec((1, RW_WIDTH), lambda i: (0, 0))
    return pl.pallas_call(
        _rw_post_body,
        grid=(rows // tt,),
        in_specs=[row_spec, row_spec, row_spec, vec_spec, vec_spec,
                  pl.BlockSpec((RW_WIDTH, RW_WIDTH), lambda i: (0, 0))],
        out_specs=row_spec,
        out_shape=jax.ShapeDtypeStruct((rows, RW_WIDTH), BF16),
        compiler_params=_params("parallel"),
        name="rw_post",
    )(o, bonus, g, lnx_w, lnx_b, ones_bd)


def rwkv7(zb, z_prev, wkv0, p, batch, seq):
    r, w, km, kk, bm, v, g, bonus = rw_prep(zb[0], zb[1], z_prev, p, batch, seq)
    bh = batch * RW_HEADS
    vpar = 128 // bh
    nv = RW_HEAD_DIM // vpar
    s0 = wkv0.reshape(batch, RW_HEADS, nv, vpar, RW_HEAD_DIM).transpose(2, 4, 3, 0, 1).reshape(
        nv, RW_HEAD_DIM, 128)
    if vpar == 2 and seq % 128 == 0:
        ops = [to_scan(x, batch, seq, False) for x in (kk, w, bm, km, r)]
        o_l, s_l = rw_scan(*ops, to_scan(v, batch, seq, True), s0)
        o = from_scan(o_l, batch, seq)
    else:
        tg = seq // TIME_SUB

        def key_layout(x):
            x = x.reshape(batch, tg, TIME_SUB, RW_HEADS, RW_HEAD_DIM).transpose(1, 4, 2, 0, 3)
            return jnp.concatenate([x.reshape(tg, RW_HEAD_DIM * TIME_SUB, bh)] * vpar, axis=-1)

        v_l = v.reshape(batch, tg, TIME_SUB, RW_HEADS, nv, vpar).transpose(1, 4, 2, 5, 0, 3).reshape(
            tg, nv * TIME_SUB, 128)
        o_l, s_l = rw_scan(key_layout(kk), key_layout(w), key_layout(bm), key_layout(km), key_layout(r),
                           v_l, s0)
        o = o_l.reshape(tg, nv, TIME_SUB, vpar, batch, RW_HEADS).transpose(4, 0, 2, 5, 1, 3).reshape(
            batch * seq, RW_WIDTH)
    s_fin = s_l.reshape(nv, RW_HEAD_DIM, vpar, batch, RW_HEADS).transpose(3, 4, 0, 2, 1).reshape(
        batch, RW_HEADS, RW_HEAD_DIM, RW_HEAD_DIM)
    out = rw_post(o, bonus, g, p["lnx_w"], p["lnx_b"], p["ones_bd"])
    return out, s_fin


def _gmlp_body(u_ref, v_ref, lnw_ref, lnb_ref, ws_ref, bs_ref, o_ref, *vn_refs, inner):
    v = v_ref[...].astype(F32)
    mu = jnp.mean(v, axis=-1, keepdims=True)
    var = jnp.mean(jnp.square(v - mu), axis=-1, keepdims=True)
    vn = (v - mu) * lax.rsqrt(var + LN_EPS) * lnw_ref[...] + lnb_ref[...]
    if vn_refs:
        vn_refs[0][...] = vn
    n = v.shape[0]
    row = lax.broadcasted_iota(jnp.int32, (n, n), 0)
    col = lax.broadcasted_iota(jnp.int32, (n, n), 1)
    causal = (col <= row) & (col >= row - row % inner)
    for gi in range(C_GROUPS):
        sl = slice(gi * C_GROUP_DIM, (gi + 1) * C_GROUP_DIM)
        wg = jnp.where(causal, ws_ref[gi], 0.0).astype(BF16)
        mixed = jnp.dot(wg, vn[:, sl].astype(BF16), preferred_element_type=F32) + bs_ref[:, gi:gi + 1]
        o_ref[:, sl] = (u_ref[:, sl].astype(F32) * mixed).astype(o_ref.dtype)


def gmlp_gate(z, ln_w, ln_b, ws, bs_t, rows_per_step, inner, want_v):
    rows = z.shape[0]
    n = rows_per_step
    out_shape = [jax.ShapeDtypeStruct((rows, C_WIDTH), BF16)]
    out_specs = [pl.BlockSpec((n, C_WIDTH), lambda i: (i, 0))]
    if want_v:
        out_shape.append(jax.ShapeDtypeStruct((rows, C_WIDTH), F32))
        out_specs.append(pl.BlockSpec((n, C_WIDTH), lambda i: (i, 0)))
    res = pl.pallas_call(
        functools.partial(_gmlp_body, inner=inner),
        grid=(rows // n,),
        in_specs=[pl.BlockSpec((n, C_WIDTH), lambda i: (i, 0)),
                  pl.BlockSpec((n, C_WIDTH), lambda i: (i, 1)),
                  pl.BlockSpec((1, C_WIDTH), lambda i: (0, 0)),
                  pl.BlockSpec((1, C_WIDTH), lambda i: (0, 0)),
                  pl.BlockSpec((C_GROUPS, n, n), lambda i: (0, 0, 0)),
                  pl.BlockSpec((n, C_GROUPS), lambda i: (0, 0))],
        out_specs=out_specs,
        out_shape=out_shape,
        compiler_params=_params("parallel"),
        name="gmlp_gate",
    )(z, z, ln_w, ln_b, ws, bs_t)
    return res if want_v else (res[0], None)


def _conv_gate_body(ug_ref, uv_ref, hg_ref, hv_ref, pg_ref, pv_ref, wg_ref, wv_ref, bg_ref, bv_ref, o_ref):
    i = pl.program_id(1)

    def conv(up_ref, halo_ref, prev_ref, w_ref, b_ref):
        h2 = jnp.where(i == 0, prev_ref[...], halo_ref[6:8, :])
        return _conv3(up_ref[...], h2, w_ref, b_ref)

    gate = conv(ug_ref, hg_ref, pg_ref, wg_ref, bg_ref)
    val = conv(uv_ref, hv_ref, pv_ref, wv_ref, bv_ref)
    o_ref[...] = (jax.nn.silu(gate) * val).astype(o_ref.dtype)


def conv_gate(up, prev, conv_w, conv_b, batch, seq):
    tt = min(seq, 256)
    nt = seq // tt
    hb = tt // 8
    tn = 1024
    nj = FFN_DIM // tn

    def halo_map(off):
        return lambda b, i, j: (jnp.maximum((b * nt + i) * hb - 1, 0), j + off)

    def col_spec(shape, off):
        return pl.BlockSpec(shape, lambda b, i, j: (0, j + off))

    return pl.pallas_call(
        _conv_gate_body,
        grid=(batch, nt, nj),
        in_specs=[pl.BlockSpec((tt, tn), lambda b, i, j: (b * nt + i, j)),
                  pl.BlockSpec((tt, tn), lambda b, i, j: (b * nt + i, j + nj)),
                  pl.BlockSpec((8, tn), halo_map(0)),
                  pl.BlockSpec((8, tn), halo_map(nj)),
                  pl.BlockSpec((None, 2, tn), lambda b, i, j: (b, 0, j)),
                  pl.BlockSpec((None, 2, tn), lambda b, i, j: (b, 0, j + nj)),
                  col_spec((CONV_W, tn), 0), col_spec((CONV_W, tn), nj),
                  col_spec((1, tn), 0), col_spec((1, tn), nj)],
        out_specs=pl.BlockSpec((tt, tn), lambda b, i, j: (b * nt + i, j)),
        out_shape=jax.ShapeDtypeStruct((batch * seq, FFN_DIM), BF16),
        compiler_params=_params("parallel", "parallel", "parallel"),
        name="conv_gate",
    )(up, up, up, up, prev, prev, conv_w, conv_w, conv_b, conv_b)


def _conv3(up, h2, w_ref, b_ref):
    row = lax.broadcasted_iota(jnp.int32, up.shape, 0)
    x1 = jnp.where(row == 0, h2[1:2, :], pltpu.roll(up, 1, axis=0))
    x2 = jnp.where(row == 0, h2[0:1, :], jnp.where(row == 1, h2[1:2, :], pltpu.roll(up, 2, axis=0)))
    return b_ref[...] + (x2 * w_ref[0:1, :] + x1 * w_ref[1:2, :] + up * w_ref[2:3, :])


def _up_conv_body(x_ref, xh_ref, wg_ref, wv_ref, pg_ref, pv_ref, cwg_ref, cwv_ref, cbg_ref, cbv_ref,
                  o_ref, tg_ref, tv_ref, wgb_ref, wvb_ref, *, tiles_per_batch):
    i = pl.program_id(1)
    first = i % tiles_per_batch == 0
    tm = x_ref.shape[0]
    x = jnp.concatenate([xh_ref[...], x_ref[...]], axis=0)

    @pl.when(i == 0)
    def _():
        wgb_ref[...] = wg_ref[...].astype(BF16)
        wvb_ref[...] = wv_ref[...].astype(BF16)

    def half(w_ref, prev_ref, cw_ref, cb_ref, tail_ref):
        up_ext = jnp.dot(x, w_ref[...], preferred_element_type=F32)
        up = up_ext[16:, :]
        h2 = jnp.where(first, prev_ref[...], up_ext[14:16, :])
        tail_ref[...] = up[tm - 8:, :]
        return _conv3(up, h2, cw_ref, cb_ref)

    gate = half(wgb_ref, pg_ref, cwg_ref, cbg_ref, tg_ref)
    val = half(wvb_ref, pv_ref, cwv_ref, cbv_ref, tv_ref)
    o_ref[...] = (jax.nn.silu(gate) * val).astype(o_ref.dtype)


def ffn_up_conv(x, w_up, layer, prev, conv_w, conv_b, batch, seq):
    assert seq % 1024 == 0
    m, k = x.shape
    tm, tn = 1024, 512
    tiles_per_batch = seq // tm
    nj = FFN_DIM // tn
    hb = tm // 16

    def col_spec(shape, off):
        return pl.BlockSpec(shape, lambda j, i: (0, j + off))

    def w_spec(off):
        return pl.BlockSpec((None, k, tn), lambda j, i: (layer, 0, j + off))

    def batch_spec(rows, off):
        return pl.BlockSpec((None, rows, tn), lambda j, i: (i // tiles_per_batch, 0, j + off))

    hid, tail_g, tail_v = pl.pallas_call(
        functools.partial(_up_conv_body, tiles_per_batch=tiles_per_batch),
        grid=(nj, m // tm),
        in_specs=[pl.BlockSpec((tm, k), lambda j, i: (i, 0)),
                  pl.BlockSpec((16, k), lambda j, i: (jnp.maximum(i * hb - 1, 0), 0)),
                  w_spec(0), w_spec(nj),
                  batch_spec(2, 0), batch_spec(2, nj),
                  col_spec((CONV_W, tn), 0), col_spec((CONV_W, tn), nj),
                  col_spec((1, tn), 0), col_spec((1, tn), nj)],
        out_specs=[pl.BlockSpec((tm, tn), lambda j, i: (i, j)),
                   pl.BlockSpec((None, 8, tn), lambda j, i: (i, 0, j)),
                   pl.BlockSpec((None, 8, tn), lambda j, i: (i, 0, j))],
        out_shape=[jax.ShapeDtypeStruct((m, FFN_DIM), BF16),
                   jax.ShapeDtypeStruct((m // tm, 8, FFN_DIM), F32),
                   jax.ShapeDtypeStruct((m // tm, 8, FFN_DIM), F32)],
        scratch_shapes=[pltpu.VMEM((k, tn), BF16), pltpu.VMEM((k, tn), BF16)],
        compiler_params=_params("parallel", "arbitrary"),
        name="ffn_up_conv",
    )(x, x, w_up, w_up, prev, prev, conv_w, conv_w, conv_b, conv_b)
    tails = jnp.concatenate([tail_g, tail_v], axis=-1)
    return hid, tails.reshape(batch, tiles_per_batch, 8, 2 * FFN_DIM)[:, -1]


_LORA_SEGS = ((LORA_W, 128), (LORA_A, 128), (LORA_G, 256))


def _pad_segs(x):
    parts, start = [], 0
    for width, padded in _LORA_SEGS:
        seg = x[..., start:start + width]
        parts.append(jnp.pad(seg, [(0, 0)] * (x.ndim - 1) + [(0, padded - width)]))
        start += width
    return jnp.concatenate(parts, axis=-1)


def _pad_cols(x, segs=None):
    c = 3 * RW_WIDTH
    return jnp.concatenate([x[..., :c], _pad_segs(x[..., c:])], axis=-1)


def _unpad_cols(x):
    c = 3 * RW_WIDTH
    return jnp.concatenate([x[..., :c], x[..., c:c + LORA_W], x[..., c + 128:c + 128 + LORA_A],
                            x[..., c + 256:c + 256 + LORA_G]], axis=-1)


def _pad_rows(w, padded):
    return jnp.pad(w, ((0, padded - w.shape[0]), (0, 0)))


def kernel(x_prompt, x_sample, cache_k, cache_v, page_table, state_shift, state_wkv, state_conv, norm_mix, norm_ffn, norm_final, w_in_even, sb_bias, mu_shift, w0, w2, a0, a2, g2, k_k, k_a, r_k, lnx_w, lnx_b, w_out_even, w_in_odd, ln_v_w, ln_v_b, w_spatial, b_spatial, w_out_odd, w_up, conv_w, conv_b, w_down):
    bp, sp = x_prompt.shape[:2]
    bs, ts = x_sample.shape[:2]
    xp = x_prompt.reshape(bp * sp, D_MODEL)
    xs = x_sample.reshape(bs * ts, D_MODEL)
    groups = ((bp, sp), (bs, ts))

    head_id = jnp.arange(RW_WIDTH) // RW_HEAD_DIM
    ones_bd = (head_id[:, None] == head_id[None, :]).astype(BF16)

    k_out, v_out, sh_out, wkv_out, cv_out, chunkv_out = ([], []), ([], []), ([], []), ([], []), ([], []), []
    xs_all = [xp, xs]
    for l in range(DEPTH):
        i = l // 2
        h_all = [rmsnorm(x, norm_mix[l], BF16) for x in xs_all]
        if l % 2 == 0:
            q_all, k_all, v_all = (matmul(h_all[0], w_in_even, layer=i, col0=c * SB_WIDTH, n=SB_WIDTH,
                                          rider=h_all[1]) for c in range(3))
            zm_all = matmul(h_all[0], w_in_even, layer=i, col0=3 * SB_WIDTH, n=3 * RW_WIDTH, rider=h_all[1])
            w_lora = _pad_segs(w_in_even[i][:, 3 * SB_WIDTH + 3 * RW_WIDTH:])
            zl_all = matmul(h_all[0], w_lora, rider=h_all[1])
            mixed = []
            w2h, w2l = _split_bf16(_pad_rows(w2[i], 128))
            a2h, a2l = _split_bf16(_pad_rows(a2[i], 128))
            g2h, g2l = _split_bf16(_pad_rows(g2[i], 256))
            p = dict(mu=_pad_cols(mu_shift[i][None, :], _LORA_SEGS), w0=w0[i][None, :], a0=a0[i][None, :],
                     k_k=k_k[i][None, :], k_a=k_a[i][None, :], r_k=r_k[i].reshape(1, RW_WIDTH),
                     w2h=w2h, w2l=w2l, a2h=a2h, a2l=a2l, g2h=g2h, g2l=g2l, ones_bd=ones_bd,
                     lnx_w=lnx_w[i][None, :], lnx_b=lnx_b[i][None, :])
            for gi, (nb, nt) in enumerate(groups):
                q_rows, k_rows, v_rows, zb = q_all[gi], k_all[gi], v_all[gi], (zm_all[gi], zl_all[gi])
                if gi == 0:
                    att = sb_prompt(q_rows, k_rows, v_rows, sb_bias[i], nb, nt)
                    z_prev = jnp.zeros((nb, RW_PAD), F32)
                    wkv0 = jnp.zeros((nb, RW_HEADS, RW_HEAD_DIM, RW_HEAD_DIM), F32)
                else:
                    att = sb_sample(q_rows, k_rows, v_rows, sb_bias[i], cache_k, cache_v, i, page_table)
                    z_prev = _pad_cols(state_shift[i], _LORA_SEGS)
                    wkv0 = state_wkv[i]
                rw, s_fin = rwkv7(zb, z_prev, wkv0, p, nb, nt)
                mixed.append([att, rw])
                k_out[gi].append(k_rows.reshape(nb, nt, SB_HEADS, SB_HEAD_DIM))
                v_out[gi].append(v_rows.reshape(nb, nt, SB_HEADS, SB_HEAD_DIM))
                last_row = jnp.concatenate([zb[0].reshape(nb, nt, -1)[:, -1], zb[1].reshape(nb, nt, -1)[:, -1]],
                                           axis=-1)
                sh_out[gi].append(_unpad_cols(last_row))
                wkv_out[gi].append(s_fin)
            xs_all = list(matmul(mixed[0], w_out_even, layer=i, res=xs_all[0], rider=mixed[1],
                                 rider_res=xs_all[1]))
        else:
            lnw = ln_v_w[i][None, :]
            lnb = ln_v_b[i][None, :]
            z_all = matmul(h_all[0], w_in_odd, layer=i, act="gelu", rider=h_all[1], out_dtype=BF16)
            mixed = []
            for gi, (nb, nt) in enumerate(groups):
                z = z_all[gi]
                if gi == 0:
                    gated, _ = gmlp_gate(z, lnw, lnb, w_spatial[i], b_spatial[i].T, CHUNK, CHUNK, False)
                else:
                    eye = jnp.eye(nb, dtype=F32)
                    ws_s = w_spatial[i][:, :nt, :nt]
                    ws_bd = (eye[None, :, None, :, None] * ws_s[:, None, :, None, :]).reshape(
                        C_GROUPS, nb * nt, nb * nt)
                    bs_t = jnp.tile(b_spatial[i][:, :nt].T, (nb, 1))
                    gated, v_rows = gmlp_gate(z, lnw, lnb, ws_bd, bs_t, nb * nt, nt, True)
                    chunkv_out.append(v_rows.reshape(nb, nt, C_WIDTH))
                mixed.append(gated)
            xs_all = list(matmul(mixed[0], w_out_odd, layer=i, res=xs_all[0], rider=mixed[1],
                                 rider_res=xs_all[1]))
        hidden = []
        for gi, (nb, nt) in enumerate(groups):
            hf = rmsnorm(xs_all[gi], norm_ffn[l], BF16)
            if gi == 0:
                prev = jnp.zeros((nb, CONV_W - 1, 2 * FFN_DIM), F32)
                hid, up_tail = ffn_up_conv(hf, w_up, l, prev, conv_w[l], conv_b[l][None, :], nb, nt)
            else:
                prev = state_conv[l]
                up = matmul(hf, w_up, layer=l)
                hid = conv_gate(up, prev, conv_w[l], conv_b[l][None, :], nb, nt)
                up_tail = up.reshape(nb, nt, 2 * FFN_DIM)
            hidden.append(hid)
            cv_out[gi].append(jnp.concatenate([prev, up_tail], axis=1)[:, -(CONV_W - 1):])
        xs_all = list(matmul(hidden[0], w_down, layer=l, res=xs_all[0], rider=hidden[1], rider_res=xs_all[1]))
    y_prompt = rmsnorm(xs_all[0], norm_final, F32).reshape(bp, sp, D_MODEL)
    y_sample = rmsnorm(xs_all[1], norm_final, F32).reshape(bs, ts, D_MODEL)
    st = jnp.stack
    return (y_prompt, y_sample, st(k_out[0]), st(v_out[0]), st(k_out[1]), st(v_out[1]),
            st(sh_out[0]), st(sh_out[1]), st(wkv_out[0]), st(wkv_out[1]),
            st(cv_out[0]), st(cv_out[1]), st(chunkv_out))
```

```python
import functools

import jax
import jax.numpy as jnp
from jax import lax
from jax.experimental import pallas as pl
from jax.experimental.pallas import tpu as pltpu

F32 = jnp.float32
BF16 = jnp.bfloat16

D_MODEL = 2048
DEPTH = 4
PAGE_SIZE = 128
SB_HEAD_DIM = 128
SB_HEADS = 8
SB_WIDTH = 1024
RW_HEAD_DIM = 64
RW_WIDTH = 1024
RW_HEADS = 16
LORA_W = 64
LORA_A = 64
LORA_G = 160
RW_IN = 3 * RW_WIDTH + LORA_W + LORA_A + LORA_G
RW_LORA_PAD = 128 + 128 + 256
RW_PAD = 3 * RW_WIDTH + RW_LORA_PAD
GN_EPS = 64e-5
C_WIDTH = 2 * D_MODEL
C_GROUPS = 8
C_GROUP_DIM = C_WIDTH // C_GROUPS
CHUNK = 128
FFN_DIM = 2 * D_MODEL
CONV_W = 3
RMS_EPS = 1e-6
LN_EPS = 1e-5
LOG2E = 1.4426950408889634
SB_SCALE2 = SB_HEAD_DIM ** -0.5 * LOG2E
TIME_SUB = 8

VMEM_LIMIT_BYTES = 52 * 1024 * 1024


def _params(*sem):
    return pltpu.CompilerParams(dimension_semantics=sem, vmem_limit_bytes=VMEM_LIMIT_BYTES)


def _rmsnorm_body(x_ref, g_ref, o_ref):
    x = x_ref[...]
    ms = jnp.mean(x * x, axis=-1, keepdims=True)
    o_ref[...] = (x * lax.rsqrt(ms + RMS_EPS) * g_ref[...]).astype(o_ref.dtype)


def rmsnorm(x, g, out_dtype):
    m, d = x.shape
    tm = min(m, 512)
    return pl.pallas_call(
        _rmsnorm_body,
        grid=(m // tm,),
        in_specs=[pl.BlockSpec((tm, d), lambda i: (i, 0)),
                  pl.BlockSpec((1, d), lambda i: (0, 0))],
        out_specs=pl.BlockSpec((tm, d), lambda i: (i, 0)),
        out_shape=jax.ShapeDtypeStruct((m, d), out_dtype),
        compiler_params=_params("parallel"),
        name="rmsnorm",
    )(x, g.reshape(1, d))


def _matmul_body(*refs, act, n_parts, has_res, has_rider):
    refs = list(refs)
    wb_ref = refs.pop()
    groups = 2 if has_rider else 1
    x_refs = [refs[g * n_parts:(g + 1) * n_parts] for g in range(groups)]
    w_ref = refs[groups * n_parts]
    r_refs = refs[groups * n_parts + 1:][:groups] if has_res else [None] * groups
    o_refs = refs[-groups:]

    def product(g):
        acc, k0 = None, 0
        for x_ref in x_refs[g]:
            kp = x_ref.shape[1]
            part = jnp.dot(x_ref[...], wb_ref[k0:k0 + kp, :], preferred_element_type=F32)
            acc = part if acc is None else acc + part
            k0 += kp
        if act == "gelu":
            acc = jax.nn.gelu(acc)
        if has_res:
            acc = r_refs[g][...] + acc
        o_refs[g][...] = acc.astype(o_refs[g].dtype)

    @pl.when(pl.program_id(1) == 0)
    def _():
        wb_ref[...] = w_ref[...].astype(BF16)
        if has_rider:
            product(1)

    product(0)


def matmul(x, w, layer=None, col0=0, n=None, res=None, act=None, rider=None, rider_res=None, out_dtype=F32):
    xs = list(x) if isinstance(x, (list, tuple)) else [x]
    has_res, has_rider = res is not None, rider is not None
    groups = [xs]
    if has_rider:
        groups.append(list(rider) if isinstance(rider, (list, tuple)) else [rider])
    m = xs[0].shape[0]
    k = sum(p.shape[1] for p in xs)
    n = w.shape[-1] - col0 if n is None else n
    tm = min(m, 1024)
    tn = 1024 if (k <= 2048 and n % 1024 == 0 and col0 % 1024 == 0) else 512
    c0 = col0 // tn
    row_tiled = lambda j, i: (i, 0)
    whole = lambda j, i: (0, 0)
    in_specs, args = [], []
    for g, parts in enumerate(groups):
        for p in parts:
            rows = tm if g == 0 else p.shape[0]
            in_specs.append(pl.BlockSpec((rows, p.shape[1]), row_tiled if g == 0 else whole))
            args.append(p)
    if layer is None:
        in_specs.append(pl.BlockSpec((k, tn), lambda j, i: (0, j + c0)))
    else:
        in_specs.append(pl.BlockSpec((None, k, tn), lambda j, i: (layer, 0, j + c0)))
    args.append(w)
    out_specs = [pl.BlockSpec((tm, tn), lambda j, i: (i, j))]
    out_shape = [jax.ShapeDtypeStruct((m, n), out_dtype)]
    if has_res:
        in_specs.append(pl.BlockSpec((tm, tn), lambda j, i: (i, j)))
        args.append(res)
    if has_rider:
        mr = groups[1][0].shape[0]
        out_specs.append(pl.BlockSpec((mr, tn), lambda j, i: (0, j)))
        out_shape.append(jax.ShapeDtypeStruct((mr, n), F32))
        if has_res:
            in_specs.append(pl.BlockSpec((mr, tn), lambda j, i: (0, j)))
            args.append(rider_res)
    out = pl.pallas_call(
        functools.partial(_matmul_body, act=act, n_parts=len(xs), has_res=has_res, has_rider=has_rider),
        grid=(n // tn, m // tm),
        in_specs=in_specs,
        out_specs=out_specs,
        out_shape=out_shape,
        scratch_shapes=[pltpu.VMEM((k, tn), BF16)],
        compiler_params=_params("parallel", "arbitrary"),
        name="matmul",
    )(*args)
    return tuple(out) if has_rider else out[0]


def _tri_and_mask(n):
    row = lax.broadcasted_iota(jnp.int32, (n, n), 0)
    col = lax.broadcasted_iota(jnp.int32, (n, n), 1)
    tri = jnp.where(row > col, 1.0, 0.0).astype(BF16)
    return tri, row, col


SB_SUB = 128


def _sb_block(q, kb, vb, bias, carry, acc, tri, mask):
    z2 = lax.dot_general(q, kb, (((1,), (1,)), ((), ())), preferred_element_type=F32) + bias
    sp = jnp.log2(1.0 + jnp.exp2(jnp.minimum(z2, -z2)))
    stay_all = jnp.maximum(z2, 0.0) + sp
    stay = stay_all if mask is None else jnp.where(mask, stay_all, 0.0)
    hi = stay.astype(BF16)
    lo = (stay - hi.astype(F32)).astype(BF16)
    parts = [None] * (z2.shape[1] // SB_SUB)
    for s in reversed(range(len(parts))):
        sl = slice(s * SB_SUB, (s + 1) * SB_SUB)
        parts[s] = (carry + jnp.dot(hi[:, sl], tri, preferred_element_type=F32)
                    + jnp.dot(lo[:, sl], tri, preferred_element_type=F32))
        carry = carry + jnp.sum(stay[:, sl], axis=1, keepdims=True)
    later = parts[0] if len(parts) == 1 else jnp.concatenate(parts, axis=1)
    att = jnp.exp2(z2 - stay_all - later)
    if mask is not None:
        att = jnp.where(mask, att, 0.0)
    acc = acc + jnp.dot(att.astype(BF16), vb, preferred_element_type=F32)
    return carry, acc


SB_HEADS_PER_STEP = 2


def _sb_prompt_body(bias_ref, q_ref, k_ref, v_ref, o_ref, *, tq):
    hg = pl.program_id(1)
    i = pl.program_id(2)
    tri, _, _ = _tri_and_mask(SB_SUB)
    heads = range(SB_HEADS_PER_STEP)
    lanes = [slice(e * SB_HEAD_DIM, (e + 1) * SB_HEAD_DIM) for e in heads]
    bias = [bias_ref[hg * SB_HEADS_PER_STEP + e] * LOG2E for e in heads]
    q = [(q_ref[:, lanes[e]] * SB_SCALE2).astype(BF16) for e in heads]

    def blk(j, state, mask):
        off = pl.multiple_of(j * tq, tq)
        out = []
        for e in heads:
            kb = k_ref[pl.ds(off, tq), lanes[e]].astype(BF16)
            vb = v_ref[pl.ds(off, tq), lanes[e]].astype(BF16)
            out.append(_sb_block(q[e], kb, vb, bias[e], state[e][0], state[e][1], tri, mask))
        return tuple(out)

    row = lax.broadcasted_iota(jnp.int32, (tq, tq), 0)
    col = lax.broadcasted_iota(jnp.int32, (tq, tq), 1)
    zero = (jnp.zeros((tq, 1), F32), jnp.zeros((tq, SB_HEAD_DIM), F32))
    state = blk(i, (zero,) * SB_HEADS_PER_STEP, col < row)
    state = lax.fori_loop(0, i, lambda n, c: blk(i - 1 - n, c, None), state)
    for e in heads:
        o_ref[:, lanes[e]] = state[e][1].astype(o_ref.dtype)


def sb_prompt(q, k, v, bias, batch, seq):
    tq = 512
    nq = seq // tq
    width = SB_HEADS_PER_STEP * SB_HEAD_DIM
    seq_spec = pl.BlockSpec((seq, width), lambda b, h, i: (b, h))
    return pl.pallas_call(
        functools.partial(_sb_prompt_body, tq=tq),
        grid=(batch, SB_HEADS // SB_HEADS_PER_STEP, nq),
        in_specs=[pl.BlockSpec(memory_space=pltpu.SMEM),
                  pl.BlockSpec((tq, width), lambda b, h, i: (b * nq + i, h)),
                  seq_spec, seq_spec],
        out_specs=pl.BlockSpec((tq, width), lambda b, h, i: (b * nq + i, h)),
        out_shape=jax.ShapeDtypeStruct((batch * seq, SB_WIDTH), BF16),
        compiler_params=_params("parallel", "parallel", "arbitrary"),
        name="sb_prompt",
    )(bias, q, k, v)


SB_PAGES_PER_STEP = 16


def _sb_sample_body(pt_ref, qbd_ref, bias_ref, kn_ref, vn_ref, *rest, t_new):
    npg = SB_PAGES_PER_STEP
    k_refs, v_refs = rest[:npg], rest[npg:2 * npg]
    o_ref, carry_ref, acc_ref = rest[2 * npg:]
    j = pl.program_id(1)
    rows = qbd_ref.shape[0]
    q = qbd_ref[...]
    bias = bias_ref[...]
    tri, _, _ = _tri_and_mask(SB_SUB)

    @pl.when(j == 0)
    def _():
        row = lax.broadcasted_iota(jnp.int32, (rows, PAGE_SIZE), 0)
        col = lax.broadcasted_iota(jnp.int32, (rows, PAGE_SIZE), 1)
        mask = col < (row % t_new)
        carry, acc = _sb_block(q, kn_ref[...].astype(BF16), vn_ref[...].astype(BF16), bias[:, :PAGE_SIZE],
                               jnp.zeros((rows, 1), F32), jnp.zeros(acc_ref.shape, F32), tri, mask)
        carry_ref[...] = carry
        acc_ref[...] = acc

    def pages(refs):
        return jnp.concatenate(
            [jnp.concatenate([r[pl.ds(h, PAGE_SIZE, stride=SB_HEADS), :].astype(BF16)
                              for h in range(SB_HEADS)], axis=1) for r in refs], axis=0)

    @pl.when(j > 0)
    def _():
        carry, acc = _sb_block(q, pages(k_refs), pages(v_refs), bias,
                               carry_ref[...], acc_ref[...], tri, None)
        carry_ref[...] = carry
        acc_ref[...] = acc

    @pl.when(j == pl.num_programs(1) - 1)
    def _():
        o_ref[...] = acc_ref[...]


def sb_sample(q, k_new, v_new, bias, cache_k, cache_v, layer, page_table):
    bd, n_pages = page_table.shape
    t_new = q.shape[0] // bd
    rows = SB_HEADS * t_new
    n_pool = cache_k.shape[1]
    npg = SB_PAGES_PER_STEP
    q4 = (q * SB_SCALE2).reshape(bd, t_new, SB_HEADS, SB_HEAD_DIM).transpose(0, 2, 1, 3)
    eye = jnp.eye(SB_HEADS, dtype=F32)
    qbd = (q4[:, :, :, None, :] * eye[None, :, None, :, None]).reshape(bd, rows, SB_WIDTH).astype(BF16)
    bias_rows = jnp.broadcast_to(jnp.repeat(bias * LOG2E, t_new)[:, None], (rows, npg * PAGE_SIZE))
    pad = ((0, 0), (0, PAGE_SIZE - t_new), (0, 0))
    kn = jnp.pad(k_new.reshape(bd, t_new, SB_WIDTH), pad)
    vn = jnp.pad(v_new.reshape(bd, t_new, SB_WIDTH), pad)
    page_rows = PAGE_SIZE * SB_HEADS
    kc = cache_k.reshape(cache_k.shape[0], n_pool, page_rows, SB_HEAD_DIM)
    vc = cache_v.reshape(cache_v.shape[0], n_pool, page_rows, SB_HEAD_DIM)

    def page_spec(m):
        def page_map(b, j, pt):
            return (layer, pt[(b + 1) * n_pages - npg * jnp.maximum(j, 1) + m], 0, 0)
        return pl.BlockSpec((None, None, page_rows, SB_HEAD_DIM), page_map)

    page_specs = [page_spec(m) for m in range(npg)]
    out = pl.pallas_call(
        functools.partial(_sb_sample_body, t_new=t_new),
        grid_spec=pltpu.PrefetchScalarGridSpec(
            num_scalar_prefetch=1,
            grid=(bd, n_pages // npg + 1),
            in_specs=[pl.BlockSpec((None, rows, SB_WIDTH), lambda b, j, pt: (b, 0, 0)),
                      pl.BlockSpec((rows, npg * PAGE_SIZE), lambda b, j, pt: (0, 0)),
                      pl.BlockSpec((None, PAGE_SIZE, SB_WIDTH), lambda b, j, pt: (b, 0, 0)),
                      pl.BlockSpec((None, PAGE_SIZE, SB_WIDTH), lambda b, j, pt: (b, 0, 0))]
            + page_specs + page_specs,
            out_specs=pl.BlockSpec((None, rows, SB_WIDTH), lambda b, j, pt: (b, 0, 0)),
            scratch_shapes=[pltpu.VMEM((rows, 1), F32), pltpu.VMEM((rows, SB_WIDTH), F32)]),
        out_shape=jax.ShapeDtypeStruct((bd, rows, SB_WIDTH), F32),
        compiler_params=_params("parallel", "arbitrary"),
        name="sb_sample",
    )(page_table.reshape(-1), qbd, bias_rows, kn, vn, *([kc] * npg), *([vc] * npg))
    o5 = out.reshape(bd, SB_HEADS, t_new, SB_HEADS, SB_HEAD_DIM)
    idx = jnp.arange(SB_HEADS)
    diag = o5[:, idx, :, idx, :]
    return diag.transpose(1, 2, 0, 3).reshape(bd * t_new, SB_WIDTH).astype(BF16)


def _split_bf16(x):
    hi = x.astype(BF16)
    lo = (x - hi.astype(F32)).astype(BF16)
    return hi, lo


def _dot3(a, b_hi, b_lo):
    a_hi, a_lo = _split_bf16(a)
    return (jnp.dot(a_hi, b_hi, preferred_element_type=F32)
            + jnp.dot(a_hi, b_lo, preferred_element_type=F32)
            + jnp.dot(a_lo, b_hi, preferred_element_type=F32))


def _head_sum(x, ones_bd):
    return jnp.dot(x.astype(BF16), ones_bd, preferred_element_type=F32)


def _rw_prep_body(zm_ref, zl_ref, hm_ref, hl_ref, prev_ref, mu_ref, w0_ref, a0_ref, kk_ref_w, ka_ref, rk_ref,
                  w2h_ref, w2l_ref, a2h_ref, a2l_ref, g2h_ref, g2l_ref, ones_ref,
                  r_out, w_out, km_out, kk_out, b_out, v_out, g_out, bonus_out):
    i = pl.program_id(1)
    zb = jnp.concatenate([zm_ref[...], zl_ref[...]], axis=1)
    halo = jnp.concatenate([hm_ref[7:8, :], hl_ref[7:8, :]], axis=1)
    prev_row = jnp.where(i == 0, prev_ref[...], halo)
    row = lax.broadcasted_iota(jnp.int32, zb.shape, 0)
    shifted = jnp.where(row == 0, prev_row, pltpu.roll(zb, 1, axis=0))
    xz = zb + (shifted - zb) * mu_ref[...]
    c = RW_WIDTH
    r = xz[:, 0:c]
    k = xz[:, c:2 * c]
    v = xz[:, 2 * c:3 * c]
    zw = xz[:, 3 * c:3 * c + 128]
    za = xz[:, 3 * c + 128:3 * c + 256]
    zg = xz[:, 3 * c + 256:3 * c + 512]
    ones_bd = ones_ref[...]
    w_log = -jax.nn.softplus(-(w0_ref[...] + _dot3(jnp.tanh(zw), w2h_ref[...], w2l_ref[...]))) - 0.5
    decay = jnp.exp(-jnp.exp(w_log))
    a = jax.nn.sigmoid(a0_ref[...] + _dot3(za, a2h_ref[...], a2l_ref[...]))
    g = _dot3(jax.nn.sigmoid(zg), g2h_ref[...], g2l_ref[...])
    kk = k * kk_ref_w[...]
    kk = kk / jnp.maximum(jnp.sqrt(_head_sum(kk * kk, ones_bd)), 1e-12)
    k_mod = k * (1.0 + (a - 1.0) * ka_ref[...])
    bonus = _head_sum(r * k_mod * rk_ref[...], ones_bd) * v
    r_out[...] = r
    w_out[...] = decay
    km_out[...] = k_mod
    kk_out[...] = kk
    b_out[...] = kk * a
    v_out[...] = v
    g_out[...] = g
    bonus_out[...] = bonus


def rw_prep(z_main, z_lora, z_prev, p, batch, seq):
    tt = min(seq, 256)
    nt = seq // tt
    rows = batch * seq
    hb = tt // 8

    def row_spec(width):
        return pl.BlockSpec((tt, width), lambda b, i: (b * nt + i, 0))

    def halo_spec(width):
        return pl.BlockSpec((8, width), lambda b, i: (jnp.maximum((b * nt + i) * hb - 1, 0), 0))

    def const_spec(shape):
        return pl.BlockSpec(shape, lambda b, i: (0,) * len(shape))

    out_sds = jax.ShapeDtypeStruct((rows, RW_WIDTH), F32)
    return pl.pallas_call(
        _rw_prep_body,
        grid=(batch, nt),
        in_specs=[row_spec(3 * RW_WIDTH), row_spec(RW_LORA_PAD),
                  halo_spec(3 * RW_WIDTH), halo_spec(RW_LORA_PAD),
                  pl.BlockSpec((None, 1, RW_PAD), lambda b, i: (b, 0, 0)),
                  const_spec((1, RW_PAD)),
                  const_spec((1, RW_WIDTH)), const_spec((1, RW_WIDTH)), const_spec((1, RW_WIDTH)),
                  const_spec((1, RW_WIDTH)), const_spec((1, RW_WIDTH)),
                  const_spec((128, RW_WIDTH)), const_spec((128, RW_WIDTH)),
                  const_spec((128, RW_WIDTH)), const_spec((128, RW_WIDTH)),
                  const_spec((256, RW_WIDTH)), const_spec((256, RW_WIDTH)),
                  const_spec((RW_WIDTH, RW_WIDTH))],
        out_specs=[row_spec(RW_WIDTH)] * 8,
        out_shape=[out_sds] * 8,
        compiler_params=_params("parallel", "arbitrary"),
        name="rw_prep",
    )(z_main, z_lora, z_main, z_lora, z_prev.reshape(batch, 1, RW_PAD), p["mu"], p["w0"], p["a0"], p["k_k"],
      p["k_a"], p["r_k"],
      p["w2h"], p["w2l"], p["a2h"], p["a2l"], p["g2h"], p["g2l"], p["ones_bd"])


def _rw_scan_body(kk_ref, w_ref, b_ref, km_ref, r_ref, v_ref, s0_ref, o_ref, s_ref, ops_ref, *, nv, tc):
    @pl.when(pl.program_id(0) == 0)
    def _():
        s_ref[...] = s0_ref[...]

    kdim = s_ref.shape[1]

    def step(t, _):
        th = lax.shift_right_logical(t, 3)
        tl = lax.bitwise_and(t, TIME_SUB - 1)
        for n, ref in enumerate((kk_ref, w_ref, b_ref, km_ref, r_ref)):
            ops_ref[n] = ref[th, pl.ds(tl, kdim, stride=TIME_SUB), :]
        kk, w, bm, km, r = (ops_ref[n] for n in range(5))
        for vp in range(nv):
            s = s_ref[vp]
            sa = -jnp.sum(s * kk, axis=0, keepdims=True)
            s = s * w + sa * bm + v_ref[th, pl.ds(vp * TIME_SUB + tl, 1), :] * km
            s_ref[vp] = s
            o_ref[th, pl.ds(vp * TIME_SUB + tl, 1), :] = jnp.sum(s * r, axis=0, keepdims=True)
        return 0

    lax.fori_loop(0, tc, step, 0)


def rw_scan(kk, w, bm, km, r, v, s0):
    groups, rows, lanes = kk.shape
    kdim = rows // TIME_SUB
    nv = v.shape[1] // TIME_SUB
    t_len = groups * TIME_SUB
    tc = min(t_len, 16)
    gc = tc // TIME_SUB
    op_spec = pl.BlockSpec((gc, rows, lanes), lambda i: (i, 0, 0))
    v_spec = pl.BlockSpec((gc, nv * TIME_SUB, lanes), lambda i: (i, 0, 0))
    s_spec = pl.BlockSpec((nv, kdim, lanes), lambda i: (0, 0, 0))
    return pl.pallas_call(
        functools.partial(_rw_scan_body, nv=nv, tc=tc),
        grid=(t_len // tc,),
        in_specs=[op_spec] * 5 + [v_spec, s_spec],
        out_specs=[v_spec, s_spec],
        out_shape=[jax.ShapeDtypeStruct((groups, nv * TIME_SUB, lanes), F32),
                   jax.ShapeDtypeStruct((nv, kdim, lanes), F32)],
        scratch_shapes=[pltpu.VMEM((5, kdim, lanes), F32)],
        compiler_params=_params("arbitrary"),
        name="rw_scan",
    )(kk, w, bm, km, r, v, s0)


def _to_scan_body(x_ref, o_ref, xt_ref, *, value_rows):
    nb, tt = x_ref.shape[:2]
    nbh = nb * RW_HEADS
    for b in range(nb):
        xt_ref[b * RW_WIDTH:(b + 1) * RW_WIDTH, :] = x_ref[b].T

    def put(row, a):
        o_ref[:, row * TIME_SUB:(row + 1) * TIME_SUB, :] = a.T.reshape(tt // TIME_SUB, TIME_SUB, 128)

    if value_rows:
        for vp in range(RW_HEAD_DIM // 2):
            a0 = xt_ref[pl.ds(2 * vp, nbh, stride=RW_HEAD_DIM), :]
            a1 = xt_ref[pl.ds(2 * vp + 1, nbh, stride=RW_HEAD_DIM), :]
            put(vp, jnp.concatenate([a0, a1], axis=0))
    else:
        for c in range(RW_HEAD_DIM):
            a = xt_ref[pl.ds(c, nbh, stride=RW_HEAD_DIM), :]
            put(c, jnp.concatenate([a, a], axis=0))


def to_scan(x, batch, seq, value_rows):
    tt = 128
    rows_out = (RW_HEAD_DIM // 2 if value_rows else RW_HEAD_DIM) * TIME_SUB
    return pl.pallas_call(
        functools.partial(_to_scan_body, value_rows=value_rows),
        grid=(seq // tt,),
        in_specs=[pl.BlockSpec((batch, tt, RW_WIDTH), lambda i: (0, i, 0))],
        out_specs=pl.BlockSpec((tt // TIME_SUB, rows_out, 128), lambda i: (i, 0, 0)),
        out_shape=jax.ShapeDtypeStruct((seq // TIME_SUB, rows_out, 128), F32),
        scratch_shapes=[pltpu.VMEM((batch * RW_WIDTH, tt), F32)],
        compiler_params=_params("parallel"),
        name="to_scan",
    )(x.reshape(batch, seq, RW_WIDTH))


def _from_scan_body(o_ref, x_ref, xt_ref):
    nb, tt = x_ref.shape[:2]
    nbh = nb * RW_HEADS
    for vp in range(RW_HEAD_DIM // 2):
        a = o_ref[:, vp * TIME_SUB:(vp + 1) * TIME_SUB, :].reshape(tt, 128).T
        xt_ref[pl.ds(2 * vp, nbh, stride=RW_HEAD_DIM), :] = a[:nbh]
        xt_ref[pl.ds(2 * vp + 1, nbh, stride=RW_HEAD_DIM), :] = a[nbh:]
    for b in range(nb):
        x_ref[b] = xt_ref[b * RW_WIDTH:(b + 1) * RW_WIDTH, :].T


def from_scan(o, batch, seq):
    tt = 128
    out = pl.pallas_call(
        _from_scan_body,
        grid=(seq // tt,),
        in_specs=[pl.BlockSpec((tt // TIME_SUB, RW_HEAD_DIM // 2 * TIME_SUB, 128), lambda i: (i, 0, 0))],
        out_specs=pl.BlockSpec((batch, tt, RW_WIDTH), lambda i: (0, i, 0)),
        out_shape=jax.ShapeDtypeStruct((batch, seq, RW_WIDTH), F32),
        scratch_shapes=[pltpu.VMEM((batch * RW_WIDTH, tt), F32)],
        compiler_params=_params("parallel"),
        name="from_scan",
    )(o)
    return out.reshape(batch * seq, RW_WIDTH)


def _rw_post_body(o_ref, bonus_ref, g_ref, lw_ref, lb_ref, ones_ref, out_ref):
    o = o_ref[...]
    ones_bd = ones_ref[...]
    inv = 1.0 / RW_HEAD_DIM
    mu = _head_sum(o, ones_bd) * inv
    d = o - mu
    var = _head_sum(d * d, ones_bd) * inv
    y = d * lax.rsqrt(var + GN_EPS) * lw_ref[...] + lb_ref[...]
    out_ref[...] = ((y + bonus_ref[...]) * g_ref[...]).astype(out_ref.dtype)


def rw_post(o, bonus, g, lnx_w, lnx_b, ones_bd):
    rows = o.shape[0]
    tt = min(rows, 512)
    row_spec = pl.BlockSpec((tt, RW_WIDTH), lambda i: (i, 0))
    vec_spec = pl.BlockSpec((1, RW_WIDTH), lambda i: (0, 0))
    return pl.pallas_call(
        _rw_post_body,
        grid=(rows // tt,),
        in_specs=[row_spec, row_spec, row_spec, vec_spec, vec_spec,
                  pl.BlockSpec((RW_WIDTH, RW_WIDTH), lambda i: (0, 0))],
        out_specs=row_spec,
        out_shape=jax.ShapeDtypeStruct((rows, RW_WIDTH), BF16),
        compiler_params=_params("parallel"),
        name="rw_post",
    )(o, bonus, g, lnx_w, lnx_b, ones_bd)


def rwkv7(zb, z_prev, wkv0, p, batch, seq):
    r, w, km, kk, bm, v, g, bonus = rw_prep(zb[0], zb[1], z_prev, p, batch, seq)
    bh = batch * RW_HEADS
    vpar = 128 // bh
    nv = RW_HEAD_DIM // vpar
    s0 = wkv0.reshape(batch, RW_HEADS, nv, vpar, RW_HEAD_DIM).transpose(2, 4, 3, 0, 1).reshape(
        nv, RW_HEAD_DIM, 128)
    if vpar == 2 and seq % 128 == 0:
        ops = [to_scan(x, batch, seq, False) for x in (kk, w, bm, km, r)]
        o_l, s_l = rw_scan(*ops, to_scan(v, batch, seq, True), s0)
        o = from_scan(o_l, batch, seq)
    else:
        tg = seq // TIME_SUB

        def key_layout(x):
            x = x.reshape(batch, tg, TIME_SUB, RW_HEADS, RW_HEAD_DIM).transpose(1, 4, 2, 0, 3)
            return jnp.concatenate([x.reshape(tg, RW_HEAD_DIM * TIME_SUB, bh)] * vpar, axis=-1)

        v_l = v.reshape(batch, tg, TIME_SUB, RW_HEADS, nv, vpar).transpose(1, 4, 2, 5, 0, 3).reshape(
            tg, nv * TIME_SUB, 128)
        o_l, s_l = rw_scan(key_layout(kk), key_layout(w), key_layout(bm), key_layout(km), key_layout(r),
                           v_l, s0)
        o = o_l.reshape(tg, nv, TIME_SUB, vpar, batch, RW_HEADS).transpose(4, 0, 2, 5, 1, 3).reshape(
            batch * seq, RW_WIDTH)
    s_fin = s_l.reshape(nv, RW_HEAD_DIM, vpar, batch, RW_HEADS).transpose(3, 4, 0, 2, 1).reshape(
        batch, RW_HEADS, RW_HEAD_DIM, RW_HEAD_DIM)
    out = rw_post(o, bonus, g, p["lnx_w"], p["lnx_b"], p["ones_bd"])
    return out, s_fin


def _gmlp_body(u_ref, v_ref, lnw_ref, lnb_ref, ws_ref, bs_ref, o_ref, *vn_refs, inner):
    v = v_ref[...].astype(F32)
    mu = jnp.mean(v, axis=-1, keepdims=True)
    var = jnp.mean(jnp.square(v - mu), axis=-1, keepdims=True)
    vn = (v - mu) * lax.rsqrt(var + LN_EPS) * lnw_ref[...] + lnb_ref[...]
    if vn_refs:
        vn_refs[0][...] = vn
    n = v.shape[0]
    row = lax.broadcasted_iota(jnp.int32, (n, n), 0)
    col = lax.broadcasted_iota(jnp.int32, (n, n), 1)
    causal = (col <= row) & (col >= row - row % inner)
    for gi in range(C_GROUPS):
        sl = slice(gi * C_GROUP_DIM, (gi + 1) * C_GROUP_DIM)
        wg = jnp.where(causal, ws_ref[gi], 0.0).astype(BF16)
        mixed = jnp.dot(wg, vn[:, sl].astype(BF16), preferred_element_type=F32) + bs_ref[:, gi:gi + 1]
        o_ref[:, sl] = (u_ref[:, sl].astype(F32) * mixed).astype(o_ref.dtype)


def gmlp_gate(z, ln_w, ln_b, ws, bs_t, rows_per_step, inner, want_v):
    rows = z.shape[0]
    n = rows_per_step
    out_shape = [jax.ShapeDtypeStruct((rows, C_WIDTH), BF16)]
    out_specs = [pl.BlockSpec((n, C_WIDTH), lambda i: (i, 0))]
    if want_v:
        out_shape.append(jax.ShapeDtypeStruct((rows, C_WIDTH), F32))
        out_specs.append(pl.BlockSpec((n, C_WIDTH), lambda i: (i, 0)))
    res = pl.pallas_call(
        functools.partial(_gmlp_body, inner=inner),
        grid=(rows // n,),
        in_specs=[pl.BlockSpec((n, C_WIDTH), lambda i: (i, 0)),
                  pl.BlockSpec((n, C_WIDTH), lambda i: (i, 1)),
                  pl.BlockSpec((1, C_WIDTH), lambda i: (0, 0)),
                  pl.BlockSpec((1, C_WIDTH), lambda i: (0, 0)),
                  pl.BlockSpec((C_GROUPS, n, n), lambda i: (0, 0, 0)),
                  pl.BlockSpec((n, C_GROUPS), lambda i: (0, 0))],
        out_specs=out_specs,
        out_shape=out_shape,
        compiler_params=_params("parallel"),
        name="gmlp_gate",
    )(z, z, ln_w, ln_b, ws, bs_t)
    return res if want_v else (res[0], None)


def _conv_gate_body(ug_ref, uv_ref, hg_ref, hv_ref, pg_ref, pv_ref, wg_ref, wv_ref, bg_ref, bv_ref, o_ref):
    i = pl.program_id(1)

    def conv(up_ref, halo_ref, prev_ref, w_ref, b_ref):
        h2 = jnp.where(i == 0, prev_ref[...], halo_ref[6:8, :])
        return _conv3(up_ref[...], h2, w_ref, b_ref)

    gate = conv(ug_ref, hg_ref, pg_ref, wg_ref, bg_ref)
    val = conv(uv_ref, hv_ref, pv_ref, wv_ref, bv_ref)
    o_ref[...] = (jax.nn.silu(gate) * val).astype(o_ref.dtype)


def conv_gate(up, prev, conv_w, conv_b, batch, seq):
    tt = min(seq, 256)
    nt = seq // tt
    hb = tt // 8
    tn = 1024
    nj = FFN_DIM // tn

    def halo_map(off):
        return lambda b, i, j: (jnp.maximum((b * nt + i) * hb - 1, 0), j + off)

    def col_spec(shape, off):
        return pl.BlockSpec(shape, lambda b, i, j: (0, j + off))

    return pl.pallas_call(
        _conv_gate_body,
        grid=(batch, nt, nj),
        in_specs=[pl.BlockSpec((tt, tn), lambda b, i, j: (b * nt + i, j)),
                  pl.BlockSpec((tt, tn), lambda b, i, j: (b * nt + i, j + nj)),
                  pl.BlockSpec((8, tn), halo_map(0)),
                  pl.BlockSpec((8, tn), halo_map(nj)),
                  pl.BlockSpec((None, 2, tn), lambda b, i, j: (b, 0, j)),
                  pl.BlockSpec((None, 2, tn), lambda b, i, j: (b, 0, j + nj)),
                  col_spec((CONV_W, tn), 0), col_spec((CONV_W, tn), nj),
                  col_spec((1, tn), 0), col_spec((1, tn), nj)],
        out_specs=pl.BlockSpec((tt, tn), lambda b, i, j: (b * nt + i, j)),
        out_shape=jax.ShapeDtypeStruct((batch * seq, FFN_DIM), BF16),
        compiler_params=_params("parallel", "parallel", "parallel"),
        name="conv_gate",
    )(up, up, up, up, prev, prev, conv_w, conv_w, conv_b, conv_b)


def _conv3(up, h2, w_ref, b_ref):
    row = lax.broadcasted_iota(jnp.int32, up.shape, 0)
    x1 = jnp.where(row == 0, h2[1:2, :], pltpu.roll(up, 1, axis=0))
    x2 = jnp.where(row == 0, h2[0:1, :], jnp.where(row == 1, h2[1:2, :], pltpu.roll(up, 2, axis=0)))
    return b_ref[...] + (x2 * w_ref[0:1, :] + x1 * w_ref[1:2, :] + up * w_ref[2:3, :])


def _up_conv_body(x_ref, xh_ref, wg_ref, wv_ref, pg_ref, pv_ref, cwg_ref, cwv_ref, cbg_ref, cbv_ref,
                  o_ref, tg_ref, tv_ref, wgb_ref, wvb_ref, *, tiles_per_batch):
    i = pl.program_id(1)
    first = i % tiles_per_batch == 0
    tm = x_ref.shape[0]
    x = jnp.concatenate([xh_ref[...], x_ref[...]], axis=0)

    @pl.when(i == 0)
    def _():
        wgb_ref[...] = wg_ref[...].astype(BF16)
        wvb_ref[...] = wv_ref[...].astype(BF16)

    def half(w_ref, prev_ref, cw_ref, cb_ref, tail_ref):
        up_ext = jnp.dot(x, w_ref[...], preferred_element_type=F32)
        up = up_ext[16:, :]
        h2 = jnp.where(first, prev_ref[...], up_ext[14:16, :])
        tail_ref[...] = up[tm - 8:, :]
        return _conv3(up, h2, cw_ref, cb_ref)

    gate = half(wgb_ref, pg_ref, cwg_ref, cbg_ref, tg_ref)
    val = half(wvb_ref, pv_ref, cwv_ref, cbv_ref, tv_ref)
    o_ref[...] = (jax.nn.silu(gate) * val).astype(o_ref.dtype)


def ffn_up_conv(x, w_up, layer, prev, conv_w, conv_b, batch, seq):
    assert seq % 1024 == 0
    m, k = x.shape
    tm, tn = 1024, 512
    tiles_per_batch = seq // tm
    nj = FFN_DIM // tn
    hb = tm // 16

    def col_spec(shape, off):
        return pl.BlockSpec(shape, lambda j, i: (0, j + off))

    def w_spec(off):
        return pl.BlockSpec((None, k, tn), lambda j, i: (layer, 0, j + off))

    def batch_spec(rows, off):
        return pl.BlockSpec((None, rows, tn), lambda j, i: (i // tiles_per_batch, 0, j + off))

    hid, tail_g, tail_v = pl.pallas_call(
        functools.partial(_up_conv_body, tiles_per_batch=tiles_per_batch),
        grid=(nj, m // tm),
        in_specs=[pl.BlockSpec((tm, k), lambda j, i: (i, 0)),
                  pl.BlockSpec((16, k), lambda j, i: (jnp.maximum(i * hb - 1, 0), 0)),
                  w_spec(0), w_spec(nj),
                  batch_spec(2, 0), batch_spec(2, nj),
                  col_spec((CONV_W, tn), 0), col_spec((CONV_W, tn), nj),
                  col_spec((1, tn), 0), col_spec((1, tn), nj)],
        out_specs=[pl.BlockSpec((tm, tn), lambda j, i: (i, j)),
                   pl.BlockSpec((None, 8, tn), lambda j, i: (i, 0, j)),
                   pl.BlockSpec((None, 8, tn), lambda j, i: (i, 0, j))],
        out_shape=[jax.ShapeDtypeStruct((m, FFN_DIM), BF16),
                   jax.ShapeDtypeStruct((m // tm, 8, FFN_DIM), F32),
                   jax.ShapeDtypeStruct((m // tm, 8, FFN_DIM), F32)],
        scratch_shapes=[pltpu.VMEM((k, tn), BF16), pltpu.VMEM((k, tn), BF16)],
        compiler_params=_params("parallel", "arbitrary"),
        name="ffn_up_conv",
    )(x, x, w_up, w_up, prev, prev, conv_w, conv_w, conv_b, conv_b)
    tails = jnp.concatenate([tail_g, tail_v], axis=-1)
    return hid, tails.reshape(batch, tiles_per_batch, 8, 2 * FFN_DIM)[:, -1]


_LORA_SEGS = ((LORA_W, 128), (LORA_A, 128), (LORA_G, 256))


def _pad_segs(x):
    parts, start = [], 0
    for width, padded in _LORA_SEGS:
        seg = x[..., start:start + width]
        parts.append(jnp.pad(seg, [(0, 0)] * (x.ndim - 1) + [(0, padded - width)]))
        start += width
    return jnp.concatenate(parts, axis=-1)


def _pad_cols(x, segs=None):
    c = 3 * RW_WIDTH
    return jnp.concatenate([x[..., :c], _pad_segs(x[..., c:])], axis=-1)


def _unpad_cols(x):
    c = 3 * RW_WIDTH
    return jnp.concatenate([x[..., :c], x[..., c:c + LORA_W], x[..., c + 128:c + 128 + LORA_A],
                            x[..., c + 256:c + 256 + LORA_G]], axis=-1)


def _pad_rows(w, padded):
    return jnp.pad(w, ((0, padded - w.shape[0]), (0, 0)))


def kernel(x_prompt, x_sample, cache_k, cache_v, page_table, state_shift, state_wkv, state_conv, norm_mix, norm_ffn, norm_final, w_in_even, sb_bias, mu_shift, w0, w2, a0, a2, g2, k_k, k_a, r_k, lnx_w, lnx_b, w_out_even, w_in_odd, ln_v_w, ln_v_b, w_spatial, b_spatial, w_out_odd, w_up, conv_w, conv_b, w_down):
    bp, sp = x_prompt.shape[:2]
    bs, ts = x_sample.shape[:2]
    xp = x_prompt.reshape(bp * sp, D_MODEL)
    xs = x_sample.reshape(bs * ts, D_MODEL)
    groups = ((bp, sp), (bs, ts))

    head_id = jnp.arange(RW_WIDTH) // RW_HEAD_DIM
    ones_bd = (head_id[:, None] == head_id[None, :]).astype(BF16)

    k_out, v_out, sh_out, wkv_out, cv_out, chunkv_out = ([], []), ([], []), ([], []), ([], []), ([], []), []
    xs_all = [xp, xs]
    for l in range(DEPTH):
        i = l // 2
        h_all = [rmsnorm(x, norm_mix[l], BF16) for x in xs_all]
        if l % 2 == 0:
            q_all, k_all, v_all = (matmul(h_all[0], w_in_even, layer=i, col0=c * SB_WIDTH, n=SB_WIDTH,
                                          rider=h_all[1]) for c in range(3))
            zm_all = matmul(h_all[0], w_in_even, layer=i, col0=3 * SB_WIDTH, n=3 * RW_WIDTH, rider=h_all[1])
            w_lora = _pad_segs(w_in_even[i][:, 3 * SB_WIDTH + 3 * RW_WIDTH:])
            zl_all = matmul(h_all[0], w_lora, rider=h_all[1])
            mixed = []
            w2h, w2l = _split_bf16(_pad_rows(w2[i], 128))
            a2h, a2l = _split_bf16(_pad_rows(a2[i], 128))
            g2h, g2l = _split_bf16(_pad_rows(g2[i], 256))
            p = dict(mu=_pad_cols(mu_shift[i][None, :], _LORA_SEGS), w0=w0[i][None, :], a0=a0[i][None, :],
                     k_k=k_k[i][None, :], k_a=k_a[i][None, :], r_k=r_k[i].reshape(1, RW_WIDTH),
                     w2h=w2h, w2l=w2l, a2h=a2h, a2l=a2l, g2h=g2h, g2l=g2l, ones_bd=ones_bd,
                     lnx_w=lnx_w[i][None, :], lnx_b=lnx_b[i][None, :])
            for gi, (nb, nt) in enumerate(groups):
                q_rows, k_rows, v_rows, zb = q_all[gi], k_all[gi], v_all[gi], (zm_all[gi], zl_all[gi])
                if gi == 0:
                    att = sb_prompt(q_rows, k_rows, v_rows, sb_bias[i], nb, nt)
                    z_prev = jnp.zeros((nb, RW_PAD), F32)
                    wkv0 = jnp.zeros((nb, RW_HEADS, RW_HEAD_DIM, RW_HEAD_DIM), F32)
                else:
                    att = sb_sample(q_rows, k_rows, v_rows, sb_bias[i], cache_k, cache_v, i, page_table)
                    z_prev = _pad_cols(state_shift[i], _LORA_SEGS)
                    wkv0 = state_wkv[i]
                rw, s_fin = rwkv7(zb, z_prev, wkv0, p, nb, nt)
                mixed.append([att, rw])
                k_out[gi].append(k_rows.reshape(nb, nt, SB_HEADS, SB_HEAD_DIM))
                v_out[gi].append(v_rows.reshape(nb, nt, SB_HEADS, SB_HEAD_DIM))
                last_row = jnp.concatenate([zb[0].reshape(nb, nt, -1)[:, -1], zb[1].reshape(nb, nt, -1)[:, -1]],
                                           axis=-1)
                sh_out[gi].append(_unpad_cols(last_row))
                wkv_out[gi].append(s_fin)
            xs_all = list(matmul(mixed[0], w_out_even, layer=i, res=xs_all[0], rider=mixed[1],
                                 rider_res=xs_all[1]))
        else:
            lnw = ln_v_w[i][None, :]
            lnb = ln_v_b[i][None, :]
            z_all = matmul(h_all[0], w_in_odd, layer=i, act="gelu", rider=h_all[1], out_dtype=BF16)
            mixed = []
            for gi, (nb, nt) in enumerate(groups):
                z = z_all[gi]
                if gi == 0:
                    gated, _ = gmlp_gate(z, lnw, lnb, w_spatial[i], b_spatial[i].T, CHUNK, CHUNK, False)
                else:
                    eye = jnp.eye(nb, dtype=F32)
                    ws_s = w_spatial[i][:, :nt, :nt]
                    ws_bd = (eye[None, :, None, :, None] * ws_s[:, None, :, None, :]).reshape(
                        C_GROUPS, nb * nt, nb * nt)
                    bs_t = jnp.tile(b_spatial[i][:, :nt].T, (nb, 1))
                    gated, v_rows = gmlp_gate(z, lnw, lnb, ws_bd, bs_t, nb * nt, nt, True)
                    chunkv_out.append(v_rows.reshape(nb, nt, C_WIDTH))
                mixed.append(gated)
            xs_all = list(matmul(mixed[0], w_out_odd, layer=i, res=xs_all[0], rider=mixed[1],
                                 rider_res=xs_all[1]))
        hidden = []
        for gi, (nb, nt) in enumerate(groups):
            hf = rmsnorm(xs_all[gi], norm_ffn[l], BF16)
            if gi == 0:
                prev = jnp.zeros((nb, CONV_W - 1, 2 * FFN_DIM), F32)
                hid, up_tail = ffn_up_conv(hf, w_up, l, prev, conv_w[l], conv_b[l][None, :], nb, nt)
            else:
                prev = state_conv[l]
                up = matmul(hf, w_up, layer=l)
                hid = conv_gate(up, prev, conv_w[l], conv_b[l][None, :], nb, nt)
                up_tail = up.reshape(nb, nt, 2 * FFN_DIM)
            hidden.append(hid)
            cv_out[gi].append(jnp.concatenate([prev, up_tail], axis=1)[:, -(CONV_W - 1):])
        xs_all = list(matmul(hidden[0], w_down, layer=l, res=xs_all[0], rider=hidden[1], rider_res=xs_all[1]))
    y_prompt = rmsnorm(xs_all[0], norm_final, F32).reshape(bp, sp, D_MODEL)
    y_sample = rmsnorm(xs_all[1], norm_final, F32).reshape(bs, ts, D_MODEL)
    st = jnp.stack
    return (y_prompt, y_sample, st(k_out[0]), st(v_out[0]), st(k_out[1]), st(v_out[1]),
            st(sh_out[0]), st(sh_out[1]), st(wkv_out[0]), st(wkv_out[1]),
            st(cv_out[0]), st(cv_out[1]), st(chunkv_out))
```

```python
import functools

import jax
import jax.numpy as jnp
from jax import lax
from jax.experimental import pallas as pl
from jax.experimental.pallas import tpu as pltpu

F32 = jnp.float32
BF16 = jnp.bfloat16

D_MODEL = 2048
DEPTH = 4
PAGE_SIZE = 128
SB_HEAD_DIM = 128
SB_HEADS = 8
SB_WIDTH = 1024
RW_HEAD_DIM = 64
RW_WIDTH = 1024
RW_HEADS = 16
LORA_W = 64
LORA_A = 64
LORA_G = 160
RW_IN = 3 * RW_WIDTH + LORA_W + LORA_A + LORA_G
RW_LORA_PAD = 128 + 128 + 256
RW_PAD = 3 * RW_WIDTH + RW_LORA_PAD
GN_EPS = 64e-5
C_WIDTH = 2 * D_MODEL
C_GROUPS = 8
C_GROUP_DIM = C_WIDTH // C_GROUPS
CHUNK = 128
FFN_DIM = 2 * D_MODEL
CONV_W = 3
RMS_EPS = 1e-6
LN_EPS = 1e-5
LOG2E = 1.4426950408889634
SB_SCALE2 = SB_HEAD_DIM ** -0.5 * LOG2E
TIME_SUB = 8

VMEM_LIMIT_BYTES = 52 * 1024 * 1024


def _params(*sem):
    return pltpu.CompilerParams(dimension_semantics=sem, vmem_limit_bytes=VMEM_LIMIT_BYTES)


def _rmsnorm_body(x_ref, g_ref, o_ref):
    x = x_ref[...]
    ms = jnp.mean(x * x, axis=-1, keepdims=True)
    o_ref[...] = (x * lax.rsqrt(ms + RMS_EPS) * g_ref[...]).astype(o_ref.dtype)


def rmsnorm(x, g, out_dtype):
    m, d = x.shape
    tm = min(m, 512)
    return pl.pallas_call(
        _rmsnorm_body,
        grid=(m // tm,),
        in_specs=[pl.BlockSpec((tm, d), lambda i: (i, 0)),
                  pl.BlockSpec((1, d), lambda i: (0, 0))],
        out_specs=pl.BlockSpec((tm, d), lambda i: (i, 0)),
        out_shape=jax.ShapeDtypeStruct((m, d), out_dtype),
        compiler_params=_params("parallel"),
        name="rmsnorm",
    )(x, g.reshape(1, d))


def _matmul_body(*refs, act, n_parts, has_res, has_rider):
    refs = list(refs)
    wb_ref = refs.pop()
    groups = 2 if has_rider else 1
    x_refs = [refs[g * n_parts:(g + 1) * n_parts] for g in range(groups)]
    w_ref = refs[groups * n_parts]
    r_refs = refs[groups * n_parts + 1:][:groups] if has_res else [None] * groups
    o_refs = refs[-groups:]

    def product(g):
        acc, k0 = None, 0
        for x_ref in x_refs[g]:
            kp = x_ref.shape[1]
            part = jnp.dot(x_ref[...], wb_ref[k0:k0 + kp, :], preferred_element_type=F32)
            acc = part if acc is None else acc + part
            k0 += kp
        if act == "gelu":
            acc = jax.nn.gelu(acc)
        if has_res:
            acc = r_refs[g][...] + acc
        o_refs[g][...] = acc.astype(o_refs[g].dtype)

    @pl.when(pl.program_id(1) == 0)
    def _():
        wb_ref[...] = w_ref[...].astype(BF16)
        if has_rider:
            product(1)

    product(0)


def matmul(x, w, layer=None, col0=0, n=None, res=None, act=None, rider=None, rider_res=None, out_dtype=F32):
    xs = list(x) if isinstance(x, (list, tuple)) else [x]
    has_res, has_rider = res is not None, rider is not None
    groups = [xs]
    if has_rider:
        groups.append(list(rider) if isinstance(rider, (list, tuple)) else [rider])
    m = xs[0].shape[0]
    k = sum(p.shape[1] for p in xs)
    n = w.shape[-1] - col0 if n is None else n
    tm = min(m, 1024)
    tn = 1024 if (k <= 2048 and n % 1024 == 0 and col0 % 1024 == 0) else 512
    c0 = col0 // tn
    row_tiled = lambda j, i: (i, 0)
    whole = lambda j, i: (0, 0)
    in_specs, args = [], []
    for g, parts in enumerate(groups):
        for p in parts:
            rows = tm if g == 0 else p.shape[0]
            in_specs.append(pl.BlockSpec((rows, p.shape[1]), row_tiled if g == 0 else whole))
            args.append(p)
    if layer is None:
        in_specs.append(pl.BlockSpec((k, tn), lambda j, i: (0, j + c0)))
    else:
        in_specs.append(pl.BlockSpec((None, k, tn), lambda j, i: (layer, 0, j + c0)))
    args.append(w)
    out_specs = [pl.BlockSpec((tm, tn), lambda j, i: (i, j))]
    out_shape = [jax.ShapeDtypeStruct((m, n), out_dtype)]
    if has_res:
        in_specs.append(pl.BlockSpec((tm, tn), lambda j, i: (i, j)))
        args.append(res)
    if has_rider:
        mr = groups[1][0].shape[0]
        out_specs.append(pl.BlockSpec((mr, tn), lambda j, i: (0, j)))
        out_shape.append(jax.ShapeDtypeStruct((mr, n), F32))
        if has_res:
            in_specs.append(pl.BlockSpec((mr, tn), lambda j, i: (0, j)))
            args.append(rider_res)
    out = pl.pallas_call(
        functools.partial(_matmul_body, act=act, n_parts=len(xs), has_res=has_res, has_rider=has_rider),
        grid=(n // tn, m // tm),
        in_specs=in_specs,
        out_specs=out_specs,
        out_shape=out_shape,
        scratch_shapes=[pltpu.VMEM((k, tn), BF16)],
        compiler_params=_params("parallel", "arbitrary"),
        name="matmul",
    )(*args)
    return tuple(out) if has_rider else out[0]


def _tri_and_mask(n):
    row = lax.broadcasted_iota(jnp.int32, (n, n), 0)
    col = lax.broadcasted_iota(jnp.int32, (n, n), 1)
    tri = jnp.where(row > col, 1.0, 0.0).astype(BF16)
    return tri, row, col


SB_SUB = 128


def _sb_block(q, kb, vb, bias, carry, acc, tri, mask):
    z2 = lax.dot_general(q, kb, (((1,), (1,)), ((), ())), preferred_element_type=F32) + bias
    sp = jnp.log2(1.0 + jnp.exp2(jnp.minimum(z2, -z2)))
    stay_all = jnp.maximum(z2, 0.0) + sp
    stay = stay_all if mask is None else jnp.where(mask, stay_all, 0.0)
    hi = stay.astype(BF16)
    lo = (stay - hi.astype(F32)).astype(BF16)
    parts = [None] * (z2.shape[1] // SB_SUB)
    for s in reversed(range(len(parts))):
        sl = slice(s * SB_SUB, (s + 1) * SB_SUB)
        parts[s] = (carry + jnp.dot(hi[:, sl], tri, preferred_element_type=F32)
                    + jnp.dot(lo[:, sl], tri, preferred_element_type=F32))
        carry = carry + jnp.sum(stay[:, sl], axis=1, keepdims=True)
    later = parts[0] if len(parts) == 1 else jnp.concatenate(parts, axis=1)
    att = jnp.exp2(z2 - stay_all - later)
    if mask is not None:
        att = jnp.where(mask, att, 0.0)
    acc = acc + jnp.dot(att.astype(BF16), vb, preferred_element_type=F32)
    return carry, acc


SB_HEADS_PER_STEP = 4


def _sb_prompt_body(bias_ref, q_ref, k_ref, v_ref, o_ref, *, tq):
    hg = pl.program_id(1)
    i = pl.program_id(2)
    tri, _, _ = _tri_and_mask(SB_SUB)
    heads = range(SB_HEADS_PER_STEP)
    lanes = [slice(e * SB_HEAD_DIM, (e + 1) * SB_HEAD_DIM) for e in heads]
    bias = [bias_ref[hg * SB_HEADS_PER_STEP + e] * LOG2E for e in heads]
    q = [(q_ref[:, lanes[e]] * SB_SCALE2).astype(BF16) for e in heads]

    def blk(j, state, mask):
        off = pl.multiple_of(j * tq, tq)
        out = []
        for e in heads:
            kb = k_ref[pl.ds(off, tq), lanes[e]].astype(BF16)
            vb = v_ref[pl.ds(off, tq), lanes[e]].astype(BF16)
            out.append(_sb_block(q[e], kb, vb, bias[e], state[e][0], state[e][1], tri, mask))
        return tuple(out)

    row = lax.broadcasted_iota(jnp.int32, (tq, tq), 0)
    col = lax.broadcasted_iota(jnp.int32, (tq, tq), 1)
    zero = (jnp.zeros((tq, 1), F32), jnp.zeros((tq, SB_HEAD_DIM), F32))
    state = blk(i, (zero,) * SB_HEADS_PER_STEP, col < row)
    state = lax.fori_loop(0, i, lambda n, c: blk(i - 1 - n, c, None), state)
    for e in heads:
        o_ref[:, lanes[e]] = state[e][1].astype(o_ref.dtype)


def sb_prompt(q, k, v, bias, batch, seq):
    tq = 512
    nq = seq // tq
    width = SB_HEADS_PER_STEP * SB_HEAD_DIM
    seq_spec = pl.BlockSpec((seq, width), lambda b, h, i: (b, h))
    return pl.pallas_call(
        functools.partial(_sb_prompt_body, tq=tq),
        grid=(batch, SB_HEADS // SB_HEADS_PER_STEP, nq),
        in_specs=[pl.BlockSpec(memory_space=pltpu.SMEM),
                  pl.BlockSpec((tq, width), lambda b, h, i: (b * nq + i, h)),
                  seq_spec, seq_spec],
        out_specs=pl.BlockSpec((tq, width), lambda b, h, i: (b * nq + i, h)),
        out_shape=jax.ShapeDtypeStruct((batch * seq, SB_WIDTH), BF16),
        compiler_params=_params("parallel", "parallel", "arbitrary"),
        name="sb_prompt",
    )(bias, q, k, v)


SB_PAGES_PER_STEP = 16


def _sb_sample_body(pt_ref, qbd_ref, bias_ref, kn_ref, vn_ref, *rest, t_new):
    npg = SB_PAGES_PER_STEP
    k_refs, v_refs = rest[:npg], rest[npg:2 * npg]
    o_ref, carry_ref, acc_ref = rest[2 * npg:]
    j = pl.program_id(1)
    rows = qbd_ref.shape[0]
    q = qbd_ref[...]
    bias = bias_ref[...]
    tri, _, _ = _tri_and_mask(SB_SUB)

    @pl.when(j == 0)
    def _():
        row = lax.broadcasted_iota(jnp.int32, (rows, PAGE_SIZE), 0)
        col = lax.broadcasted_iota(jnp.int32, (rows, PAGE_SIZE), 1)
        mask = col < (row % t_new)
        carry, acc = _sb_block(q, kn_ref[...].astype(BF16), vn_ref[...].astype(BF16), bias[:, :PAGE_SIZE],
                               jnp.zeros((rows, 1), F32), jnp.zeros(acc_ref.shape, F32), tri, mask)
        carry_ref[...] = carry
        acc_ref[...] = acc

    def pages(refs):
        return jnp.concatenate(
            [jnp.concatenate([r[pl.ds(h, PAGE_SIZE, stride=SB_HEADS), :].astype(BF16)
                              for h in range(SB_HEADS)], axis=1) for r in refs], axis=0)

    @pl.when(j > 0)
    def _():
        carry, acc = _sb_block(q, pages(k_refs), pages(v_refs), bias,
                               carry_ref[...], acc_ref[...], tri, None)
        carry_ref[...] = carry
        acc_ref[...] = acc

    @pl.when(j == pl.num_programs(1) - 1)
    def _():
        o_ref[...] = acc_ref[...]


def sb_sample(q, k_new, v_new, bias, cache_k, cache_v, layer, page_table):
    bd, n_pages = page_table.shape
    t_new = q.shape[0] // bd
    rows = SB_HEADS * t_new
    n_pool = cache_k.shape[1]
    npg = SB_PAGES_PER_STEP
    q4 = (q * SB_SCALE2).reshape(bd, t_new, SB_HEADS, SB_HEAD_DIM).transpose(0, 2, 1, 3)
    eye = jnp.eye(SB_HEADS, dtype=F32)
    qbd = (q4[:, :, :, None, :] * eye[None, :, None, :, None]).reshape(bd, rows, SB_WIDTH).astype(BF16)
    bias_rows = jnp.broadcast_to(jnp.repeat(bias * LOG2E, t_new)[:, None], (rows, npg * PAGE_SIZE))
    pad = ((0, 0), (0, PAGE_SIZE - t_new), (0, 0))
    kn = jnp.pad(k_new.reshape(bd, t_new, SB_WIDTH), pad)
    vn = jnp.pad(v_new.reshape(bd, t_new, SB_WIDTH), pad)
    page_rows = PAGE_SIZE * SB_HEADS
    kc = cache_k.reshape(cache_k.shape[0], n_pool, page_rows, SB_HEAD_DIM)
    vc = cache_v.reshape(cache_v.shape[0], n_pool, page_rows, SB_HEAD_DIM)

    def page_spec(m):
        def page_map(b, j, pt):
            return (layer, pt[(b + 1) * n_pages - npg * jnp.maximum(j, 1) + m], 0, 0)
        return pl.BlockSpec((None, None, page_rows, SB_HEAD_DIM), page_map)

    page_specs = [page_spec(m) for m in range(npg)]
    out = pl.pallas_call(
        functools.partial(_sb_sample_body, t_new=t_new),
        grid_spec=pltpu.PrefetchScalarGridSpec(
            num_scalar_prefetch=1,
            grid=(bd, n_pages // npg + 1),
            in_specs=[pl.BlockSpec((None, rows, SB_WIDTH), lambda b, j, pt: (b, 0, 0)),
                      pl.BlockSpec((rows, npg * PAGE_SIZE), lambda b, j, pt: (0, 0)),
                      pl.BlockSpec((None, PAGE_SIZE, SB_WIDTH), lambda b, j, pt: (b, 0, 0)),
                      pl.BlockSpec((None, PAGE_SIZE, SB_WIDTH), lambda b, j, pt: (b, 0, 0))]
            + page_specs + page_specs,
            out_specs=pl.BlockSpec((None, rows, SB_WIDTH), lambda b, j, pt: (b, 0, 0)),
            scratch_shapes=[pltpu.VMEM((rows, 1), F32), pltpu.VMEM((rows, SB_WIDTH), F32)]),
        out_shape=jax.ShapeDtypeStruct((bd, rows, SB_WIDTH), F32),
        compiler_params=_params("parallel", "arbitrary"),
        name="sb_sample",
    )(page_table.reshape(-1), qbd, bias_rows, kn, vn, *([kc] * npg), *([vc] * npg))
    o5 = out.reshape(bd, SB_HEADS, t_new, SB_HEADS, SB_HEAD_DIM)
    idx = jnp.arange(SB_HEADS)
    diag = o5[:, idx, :, idx, :]
    return diag.transpose(1, 2, 0, 3).reshape(bd * t_new, SB_WIDTH).astype(BF16)


def _split_bf16(x):
    hi = x.astype(BF16)
    lo = (x - hi.astype(F32)).astype(BF16)
    return hi, lo


def _dot3(a, b_hi, b_lo):
    a_hi, a_lo = _split_bf16(a)
    return (jnp.dot(a_hi, b_hi, preferred_element_type=F32)
            + jnp.dot(a_hi, b_lo, preferred_element_type=F32)
            + jnp.dot(a_lo, b_hi, preferred_element_type=F32))


def _head_sum(x, ones_bd):
    return jnp.dot(x.astype(BF16), ones_bd, preferred_element_type=F32)


def _rw_prep_body(zm_ref, zl_ref, hm_ref, hl_ref, prev_ref, mu_ref, w0_ref, a0_ref, kk_ref_w, ka_ref, rk_ref,
                  w2h_ref, w2l_ref, a2h_ref, a2l_ref, g2h_ref, g2l_ref, ones_ref,
                  r_out, w_out, km_out, kk_out, b_out, v_out, g_out, bonus_out):
    i = pl.program_id(1)
    zb = jnp.concatenate([zm_ref[...], zl_ref[...]], axis=1)
    halo = jnp.concatenate([hm_ref[7:8, :], hl_ref[7:8, :]], axis=1)
    prev_row = jnp.where(i == 0, prev_ref[...], halo)
    row = lax.broadcasted_iota(jnp.int32, zb.shape, 0)
    shifted = jnp.where(row == 0, prev_row, pltpu.roll(zb, 1, axis=0))
    xz = zb + (shifted - zb) * mu_ref[...]
    c = RW_WIDTH
    r = xz[:, 0:c]
    k = xz[:, c:2 * c]
    v = xz[:, 2 * c:3 * c]
    zw = xz[:, 3 * c:3 * c + 128]
    za = xz[:, 3 * c + 128:3 * c + 256]
    zg = xz[:, 3 * c + 256:3 * c + 512]
    ones_bd = ones_ref[...]
    w_log = -jax.nn.softplus(-(w0_ref[...] + _dot3(jnp.tanh(zw), w2h_ref[...], w2l_ref[...]))) - 0.5
    decay = jnp.exp(-jnp.exp(w_log))
    a = jax.nn.sigmoid(a0_ref[...] + _dot3(za, a2h_ref[...], a2l_ref[...]))
    g = _dot3(jax.nn.sigmoid(zg), g2h_ref[...], g2l_ref[...])
    kk = k * kk_ref_w[...]
    kk = kk / jnp.maximum(jnp.sqrt(_head_sum(kk * kk, ones_bd)), 1e-12)
    k_mod = k * (1.0 + (a - 1.0) * ka_ref[...])
    bonus = _head_sum(r * k_mod * rk_ref[...], ones_bd) * v
    r_out[...] = r
    w_out[...] = decay
    km_out[...] = k_mod
    kk_out[...] = kk
    b_out[...] = kk * a
    v_out[...] = v
    g_out[...] = g
    bonus_out[...] = bonus


def rw_prep(z_main, z_lora, z_prev, p, batch, seq):
    tt = min(seq, 256)
    nt = seq // tt
    rows = batch * seq
    hb = tt // 8

    def row_spec(width):
        return pl.BlockSpec((tt, width), lambda b, i: (b * nt + i, 0))

    def halo_spec(width):
        return pl.BlockSpec((8, width), lambda b, i: (jnp.maximum((b * nt + i) * hb - 1, 0), 0))

    def const_spec(shape):
        return pl.BlockSpec(shape, lambda b, i: (0,) * len(shape))

    out_sds = jax.ShapeDtypeStruct((rows, RW_WIDTH), F32)
    return pl.pallas_call(
        _rw_prep_body,
        grid=(batch, nt),
        in_specs=[row_spec(3 * RW_WIDTH), row_spec(RW_LORA_PAD),
                  halo_spec(3 * RW_WIDTH), halo_spec(RW_LORA_PAD),
                  pl.BlockSpec((None, 1, RW_PAD), lambda b, i: (b, 0, 0)),
                  const_spec((1, RW_PAD)),
                  const_spec((1, RW_WIDTH)), const_spec((1, RW_WIDTH)), const_spec((1, RW_WIDTH)),
                  const_spec((1, RW_WIDTH)), const_spec((1, RW_WIDTH)),
                  const_spec((128, RW_WIDTH)), const_spec((128, RW_WIDTH)),
                  const_spec((128, RW_WIDTH)), const_spec((128, RW_WIDTH)),
                  const_spec((256, RW_WIDTH)), const_spec((256, RW_WIDTH)),
                  const_spec((RW_WIDTH, RW_WIDTH))],
        out_specs=[row_spec(RW_WIDTH)] * 8,
        out_shape=[out_sds] * 8,
        compiler_params=_params("parallel", "arbitrary"),
        name="rw_prep",
    )(z_main, z_lora, z_main, z_lora, z_prev.reshape(batch, 1, RW_PAD), p["mu"], p["w0"], p["a0"], p["k_k"],
      p["k_a"], p["r_k"],
      p["w2h"], p["w2l"], p["a2h"], p["a2l"], p["g2h"], p["g2l"], p["ones_bd"])


def _rw_scan_body(kk_ref, w_ref, b_ref, km_ref, r_ref, v_ref, s0_ref, o_ref, s_ref, ops_ref, *, nv, tc):
    @pl.when(pl.program_id(0) == 0)
    def _():
        s_ref[...] = s0_ref[...]

    kdim = s_ref.shape[1]

    def step(t, _):
        th = lax.shift_right_logical(t, 3)
        tl = lax.bitwise_and(t, TIME_SUB - 1)
        for n, ref in enumerate((kk_ref, w_ref, b_ref, km_ref, r_ref)):
            ops_ref[n] = ref[th, pl.ds(tl, kdim, stride=TIME_SUB), :]
        kk, w, bm, km, r = (ops_ref[n] for n in range(5))
        for vp in range(nv):
            s = s_ref[vp]
            sa = -jnp.sum(s * kk, axis=0, keepdims=True)
            s = s * w + sa * bm + v_ref[th, pl.ds(vp * TIME_SUB + tl, 1), :] * km
            s_ref[vp] = s
            o_ref[th, pl.ds(vp * TIME_SUB + tl, 1), :] = jnp.sum(s * r, axis=0, keepdims=True)
        return 0

    lax.fori_loop(0, tc, step, 0)


def rw_scan(kk, w, bm, km, r, v, s0):
    groups, rows, lanes = kk.shape
    kdim = rows // TIME_SUB
    nv = v.shape[1] // TIME_SUB
    t_len = groups * TIME_SUB
    tc = min(t_len, 16)
    gc = tc // TIME_SUB
    op_spec = pl.BlockSpec((gc, rows, lanes), lambda i: (i, 0, 0))
    v_spec = pl.BlockSpec((gc, nv * TIME_SUB, lanes), lambda i: (i, 0, 0))
    s_spec = pl.BlockSpec((nv, kdim, lanes), lambda i: (0, 0, 0))
    return pl.pallas_call(
        functools.partial(_rw_scan_body, nv=nv, tc=tc),
        grid=(t_len // tc,),
        in_specs=[op_spec] * 5 + [v_spec, s_spec],
        out_specs=[v_spec, s_spec],
        out_shape=[jax.ShapeDtypeStruct((groups, nv * TIME_SUB, lanes), F32),
                   jax.ShapeDtypeStruct((nv, kdim, lanes), F32)],
        scratch_shapes=[pltpu.VMEM((5, kdim, lanes), F32)],
        compiler_params=_params("arbitrary"),
        name="rw_scan",
    )(kk, w, bm, km, r, v, s0)


def _to_scan_body(x_ref, o_ref, xt_ref, *, value_rows):
    nb, tt = x_ref.shape[:2]
    nbh = nb * RW_HEADS
    for b in range(nb):
        xt_ref[b * RW_WIDTH:(b + 1) * RW_WIDTH, :] = x_ref[b].T

    def put(row, a):
        o_ref[:, row * TIME_SUB:(row + 1) * TIME_SUB, :] = a.T.reshape(tt // TIME_SUB, TIME_SUB, 128)

    if value_rows:
        for vp in range(RW_HEAD_DIM // 2):
            a0 = xt_ref[pl.ds(2 * vp, nbh, stride=RW_HEAD_DIM), :]
            a1 = xt_ref[pl.ds(2 * vp + 1, nbh, stride=RW_HEAD_DIM), :]
            put(vp, jnp.concatenate([a0, a1], axis=0))
    else:
        for c in range(RW_HEAD_DIM):
            a = xt_ref[pl.ds(c, nbh, stride=RW_HEAD_DIM), :]
            put(c, jnp.concatenate([a, a], axis=0))


def to_scan(x, batch, seq, value_rows):
    tt = 128
    rows_out = (RW_HEAD_DIM // 2 if value_rows else RW_HEAD_DIM) * TIME_SUB
    return pl.pallas_call(
        functools.partial(_to_scan_body, value_rows=value_rows),
        grid=(seq // tt,),
        in_specs=[pl.BlockSpec((batch, tt, RW_WIDTH), lambda i: (0, i, 0))],
        out_specs=pl.BlockSpec((tt // TIME_SUB, rows_out, 128), lambda i: (i, 0, 0)),
        out_shape=jax.ShapeDtypeStruct((seq // TIME_SUB, rows_out, 128), F32),
        scratch_shapes=[pltpu.VMEM((batch * RW_WIDTH, tt), F32)],
        compiler_params=_params("parallel"),
        name="to_scan",
    )(x.reshape(batch, seq, RW_WIDTH))


def _from_scan_body(o_ref, x_ref, xt_ref):
    nb, tt = x_ref.shape[:2]
    nbh = nb * RW_HEADS
    for vp in range(RW_HEAD_DIM // 2):
        a = o_ref[:, vp * TIME_SUB:(vp + 1) * TIME_SUB, :].reshape(tt, 128).T
        xt_ref[pl.ds(2 * vp, nbh, stride=RW_HEAD_DIM), :] = a[:nbh]
        xt_ref[pl.ds(2 * vp + 1, nbh, stride=RW_HEAD_DIM), :] = a[nbh:]
    for b in range(nb):
        x_ref[b] = xt_ref[b * RW_WIDTH:(b + 1) * RW_WIDTH, :].T


def from_scan(o, batch, seq):
    tt = 128
    out = pl.pallas_call(
        _from_scan_body,
        grid=(seq // tt,),
        in_specs=[pl.BlockSpec((tt // TIME_SUB, RW_HEAD_DIM // 2 * TIME_SUB, 128), lambda i: (i, 0, 0))],
        out_specs=pl.BlockSpec((batch, tt, RW_WIDTH), lambda i: (0, i, 0)),
        out_shape=jax.ShapeDtypeStruct((batch, seq, RW_WIDTH), F32),
        scratch_shapes=[pltpu.VMEM((batch * RW_WIDTH, tt), F32)],
        compiler_params=_params("parallel"),
        name="from_scan",
    )(o)
    return out.reshape(batch * seq, RW_WIDTH)


def _rw_post_body(o_ref, bonus_ref, g_ref, lw_ref, lb_ref, ones_ref, out_ref):
    o = o_ref[...]
    ones_bd = ones_ref[...]
    inv = 1.0 / RW_HEAD_DIM
    mu = _head_sum(o, ones_bd) * inv
    d = o - mu
    var = _head_sum(d * d, ones_bd) * inv
    y = d * lax.rsqrt(var + GN_EPS) * lw_ref[...] + lb_ref[...]
    out_ref[...] = ((y + bonus_ref[...]) * g_ref[...]).astype(out_ref.dtype)


def rw_post(o, bonus, g, lnx_w, lnx_b, ones_bd):
    rows = o.shape[0]
    tt = min(rows, 512)
    row_spec = pl.BlockSpec((tt, RW_WIDTH), lambda i: (i, 0))
    vec_spec = pl.BlockSpec((1, RW_WIDTH), lambda i: (0, 0))
    return pl.pallas_call(
        _rw_post_body,
        grid=(rows // tt,),
        in_specs=[row_spec, row_spec, row_spec, vec_spec, vec_spec,
                  pl.BlockSpec((RW_WIDTH, RW_WIDTH), lambda i: (0, 0))],
        out_specs=row_spec,
        out_shape=jax.ShapeDtypeStruct((rows, RW_WIDTH), BF16),
        compiler_params=_params("parallel"),
        name="rw_post",
    )(o, bonus, g, lnx_w, lnx_b, ones_bd)


def rwkv7(zb, z_prev, wkv0, p, batch, seq):
    r, w, km, kk, bm, v, g, bonus = rw_prep(zb[0], zb[1], z_prev, p, batch, seq)
    bh = batch * RW_HEADS
    vpar = 128 // bh
    nv = RW_HEAD_DIM // vpar
    s0 = wkv0.reshape(batch, RW_HEADS, nv, vpar, RW_HEAD_DIM).transpose(2, 4, 3, 0, 1).reshape(
        nv, RW_HEAD_DIM, 128)
    if vpar == 2 and seq % 128 == 0:
        ops = [to_scan(x, batch, seq, False) for x in (kk, w, bm, km, r)]
        o_l, s_l = rw_scan(*ops, to_scan(v, batch, seq, True), s0)
        o = from_scan(o_l, batch, seq)
    else:
        tg = seq // TIME_SUB

        def key_layout(x):
            x = x.reshape(batch, tg, TIME_SUB, RW_HEADS, RW_HEAD_DIM).transpose(1, 4, 2, 0, 3)
            return jnp.concatenate([x.reshape(tg, RW_HEAD_DIM * TIME_SUB, bh)] * vpar, axis=-1)

        v_l = v.reshape(batch, tg, TIME_SUB, RW_HEADS, nv, vpar).transpose(1, 4, 2, 5, 0, 3).reshape(
            tg, nv * TIME_SUB, 128)
        o_l, s_l = rw_scan(key_layout(kk), key_layout(w), key_layout(bm), key_layout(km), key_layout(r),
                           v_l, s0)
        o = o_l.reshape(tg, nv, TIME_SUB, vpar, batch, RW_HEADS).transpose(4, 0, 2, 5, 1, 3).reshape(
            batch * seq, RW_WIDTH)
    s_fin = s_l.reshape(nv, RW_HEAD_DIM, vpar, batch, RW_HEADS).transpose(3, 4, 0, 2, 1).reshape(
        batch, RW_HEADS, RW_HEAD_DIM, RW_HEAD_DIM)
    out = rw_post(o, bonus, g, p["lnx_w"], p["lnx_b"], p["ones_bd"])
    return out, s_fin


def _gmlp_body(u_ref, v_ref, lnw_ref, lnb_ref, ws_ref, bs_ref, o_ref, *vn_refs, inner):
    v = v_ref[...].astype(F32)
    mu = jnp.mean(v, axis=-1, keepdims=True)
    var = jnp.mean(jnp.square(v - mu), axis=-1, keepdims=True)
    vn = (v - mu) * lax.rsqrt(var + LN_EPS) * lnw_ref[...] + lnb_ref[...]
    if vn_refs:
        vn_refs[0][...] = vn
    n = v.shape[0]
    row = lax.broadcasted_iota(jnp.int32, (n, n), 0)
    col = lax.broadcasted_iota(jnp.int32, (n, n), 1)
    causal = (col <= row) & (col >= row - row % inner)
    for gi in range(C_GROUPS):
        sl = slice(gi * C_GROUP_DIM, (gi + 1) * C_GROUP_DIM)
        wg = jnp.where(causal, ws_ref[gi], 0.0).astype(BF16)
        mixed = jnp.dot(wg, vn[:, sl].astype(BF16), preferred_element_type=F32) + bs_ref[:, gi:gi + 1]
        o_ref[:, sl] = (u_ref[:, sl].astype(F32) * mixed).astype(o_ref.dtype)


def gmlp_gate(z, ln_w, ln_b, ws, bs_t, rows_per_step, inner, want_v):
    rows = z.shape[0]
    n = rows_per_step
    out_shape = [jax.ShapeDtypeStruct((rows, C_WIDTH), BF16)]
    out_specs = [pl.BlockSpec((n, C_WIDTH), lambda i: (i, 0))]
    if want_v:
        out_shape.append(jax.ShapeDtypeStruct((rows, C_WIDTH), F32))
        out_specs.append(pl.BlockSpec((n, C_WIDTH), lambda i: (i, 0)))
    res = pl.pallas_call(
        functools.partial(_gmlp_body, inner=inner),
        grid=(rows // n,),
        in_specs=[pl.BlockSpec((n, C_WIDTH), lambda i: (i, 0)),
                  pl.BlockSpec((n, C_WIDTH), lambda i: (i, 1)),
                  pl.BlockSpec((1, C_WIDTH), lambda i: (0, 0)),
                  pl.BlockSpec((1, C_WIDTH), lambda i: (0, 0)),
                  pl.BlockSpec((C_GROUPS, n, n), lambda i: (0, 0, 0)),
                  pl.BlockSpec((n, C_GROUPS), lambda i: (0, 0))],
        out_specs=out_specs,
        out_shape=out_shape,
        compiler_params=_params("parallel"),
        name="gmlp_gate",
    )(z, z, ln_w, ln_b, ws, bs_t)
    return res if want_v else (res[0], None)


def _conv_gate_body(ug_ref, uv_ref, hg_ref, hv_ref, pg_ref, pv_ref, wg_ref, wv_ref, bg_ref, bv_ref, o_ref):
    i = pl.program_id(1)

    def conv(up_ref, halo_ref, prev_ref, w_ref, b_ref):
        h2 = jnp.where(i == 0, prev_ref[...], halo_ref[6:8, :])
        return _conv3(up_ref[...], h2, w_ref, b_ref)

    gate = conv(ug_ref, hg_ref, pg_ref, wg_ref, bg_ref)
    val = conv(uv_ref, hv_ref, pv_ref, wv_ref, bv_ref)
    o_ref[...] = (jax.nn.silu(gate) * val).astype(o_ref.dtype)


def conv_gate(up, prev, conv_w, conv_b, batch, seq):
    tt = min(seq, 256)
    nt = seq // tt
    hb = tt // 8
    tn = 1024
    nj = FFN_DIM // tn

    def halo_map(off):
        return lambda b, i, j: (jnp.maximum((b * nt + i) * hb - 1, 0), j + off)

    def col_spec(shape, off):
        return pl.BlockSpec(shape, lambda b, i, j: (0, j + off))

    return pl.pallas_call(
        _conv_gate_body,
        grid=(batch, nt, nj),
        in_specs=[pl.BlockSpec((tt, tn), lambda b, i, j: (b * nt + i, j)),
                  pl.BlockSpec((tt, tn), lambda b, i, j: (b * nt + i, j + nj)),
                  pl.BlockSpec((8, tn), halo_map(0)),
                  pl.BlockSpec((8, tn), halo_map(nj)),
                  pl.BlockSpec((None, 2, tn), lambda b, i, j: (b, 0, j)),
                  pl.BlockSpec((None, 2, tn), lambda b, i, j: (b, 0, j + nj)),
                  col_spec((CONV_W, tn), 0), col_spec((CONV_W, tn), nj),
                  col_spec((1, tn), 0), col_spec((1, tn), nj)],
        out_specs=pl.BlockSpec((tt, tn), lambda b, i, j: (b * nt + i, j)),
        out_shape=jax.ShapeDtypeStruct((batch * seq, FFN_DIM), BF16),
        compiler_params=_params("parallel", "parallel", "parallel"),
        name="conv_gate",
    )(up, up, up, up, prev, prev, conv_w, conv_w, conv_b, conv_b)


def _conv3(up, h2, w_ref, b_ref):
    row = lax.broadcasted_iota(jnp.int32, up.shape, 0)
    x1 = jnp.where(row == 0, h2[1:2, :], pltpu.roll(up, 1, axis=0))
    x2 = jnp.where(row == 0, h2[0:1, :], jnp.where(row == 1, h2[1:2, :], pltpu.roll(up, 2, axis=0)))
    return b_ref[...] + (x2 * w_ref[0:1, :] + x1 * w_ref[1:2, :] + up * w_ref[2:3, :])


def _up_conv_body(x_ref, xh_ref, wg_ref, wv_ref, pg_ref, pv_ref, cwg_ref, cwv_ref, cbg_ref, cbv_ref,
                  o_ref, tg_ref, tv_ref, wgb_ref, wvb_ref, *, tiles_per_batch):
    i = pl.program_id(1)
    first = i % tiles_per_batch == 0
    tm = x_ref.shape[0]
    x = jnp.concatenate([xh_ref[...], x_ref[...]], axis=0)

    @pl.when(i == 0)
    def _():
        wgb_ref[...] = wg_ref[...].astype(BF16)
        wvb_ref[...] = wv_ref[...].astype(BF16)

    def half(w_ref, prev_ref, cw_ref, cb_ref, tail_ref):
        up_ext = jnp.dot(x, w_ref[...], preferred_element_type=F32)
        up = up_ext[16:, :]
        h2 = jnp.where(first, prev_ref[...], up_ext[14:16, :])
        tail_ref[...] = up[tm - 8:, :]
        return _conv3(up, h2, cw_ref, cb_ref)

    gate = half(wgb_ref, pg_ref, cwg_ref, cbg_ref, tg_ref)
    val = half(wvb_ref, pv_ref, cwv_ref, cbv_ref, tv_ref)
    o_ref[...] = (jax.nn.silu(gate) * val).astype(o_ref.dtype)


def ffn_up_conv(x, w_up, layer, prev, conv_w, conv_b, batch, seq):
    assert seq % 1024 == 0
    m, k = x.shape
    tm, tn = 1024, 512
    tiles_per_batch = seq // tm
    nj = FFN_DIM // tn
    hb = tm // 16

    def col_spec(shape, off):
        return pl.BlockSpec(shape, lambda j, i: (0, j + off))

    def w_spec(off):
        return pl.BlockSpec((None, k, tn), lambda j, i: (layer, 0, j + off))

    def batch_spec(rows, off):
        return pl.BlockSpec((None, rows, tn), lambda j, i: (i // tiles_per_batch, 0, j + off))

    hid, tail_g, tail_v = pl.pallas_call(
        functools.partial(_up_conv_body, tiles_per_batch=tiles_per_batch),
        grid=(nj, m // tm),
        in_specs=[pl.BlockSpec((tm, k), lambda j, i: (i, 0)),
                  pl.BlockSpec((16, k), lambda j, i: (jnp.maximum(i * hb - 1, 0), 0)),
                  w_spec(0), w_spec(nj),
                  batch_spec(2, 0), batch_spec(2, nj),
                  col_spec((CONV_W, tn), 0), col_spec((CONV_W, tn), nj),
                  col_spec((1, tn), 0), col_spec((1, tn), nj)],
        out_specs=[pl.BlockSpec((tm, tn), lambda j, i: (i, j)),
                   pl.BlockSpec((None, 8, tn), lambda j, i: (i, 0, j)),
                   pl.BlockSpec((None, 8, tn), lambda j, i: (i, 0, j))],
        out_shape=[jax.ShapeDtypeStruct((m, FFN_DIM), BF16),
                   jax.ShapeDtypeStruct((m // tm, 8, FFN_DIM), F32),
                   jax.ShapeDtypeStruct((m // tm, 8, FFN_DIM), F32)],
        scratch_shapes=[pltpu.VMEM((k, tn), BF16), pltpu.VMEM((k, tn), BF16)],
        compiler_params=_params("parallel", "arbitrary"),
        name="ffn_up_conv",
    )(x, x, w_up, w_up, prev, prev, conv_w, conv_w, conv_b, conv_b)
    tails = jnp.concatenate([tail_g, tail_v], axis=-1)
    return hid, tails.reshape(batch, tiles_per_batch, 8, 2 * FFN_DIM)[:, -1]


_LORA_SEGS = ((LORA_W, 128), (LORA_A, 128), (LORA_G, 256))


def _pad_segs(x):
    parts, start = [], 0
    for width, padded in _LORA_SEGS:
        seg = x[..., start:start + width]
        parts.append(jnp.pad(seg, [(0, 0)] * (x.ndim - 1) + [(0, padded - width)]))
        start += width
    return jnp.concatenate(parts, axis=-1)


def _pad_cols(x, segs=None):
    c = 3 * RW_WIDTH
    return jnp.concatenate([x[..., :c], _pad_segs(x[..., c:])], axis=-1)


def _unpad_cols(x):
    c = 3 * RW_WIDTH
    return jnp.concatenate([x[..., :c], x[..., c:c + LORA_W], x[..., c + 128:c + 128 + LORA_A],
                            x[..., c + 256:c + 256 + LORA_G]], axis=-1)


def _pad_rows(w, padded):
    return jnp.pad(w, ((0, padded - w.shape[0]), (0, 0)))


def kernel(x_prompt, x_sample, cache_k, cache_v, page_table, state_shift, state_wkv, state_conv, norm_mix, norm_ffn, norm_final, w_in_even, sb_bias, mu_shift, w0, w2, a0, a2, g2, k_k, k_a, r_k, lnx_w, lnx_b, w_out_even, w_in_odd, ln_v_w, ln_v_b, w_spatial, b_spatial, w_out_odd, w_up, conv_w, conv_b, w_down):
    bp, sp = x_prompt.shape[:2]
    bs, ts = x_sample.shape[:2]
    xp = x_prompt.reshape(bp * sp, D_MODEL)
    xs = x_sample.reshape(bs * ts, D_MODEL)
    groups = ((bp, sp), (bs, ts))

    head_id = jnp.arange(RW_WIDTH) // RW_HEAD_DIM
    ones_bd = (head_id[:, None] == head_id[None, :]).astype(BF16)

    k_out, v_out, sh_out, wkv_out, cv_out, chunkv_out = ([], []), ([], []), ([], []), ([], []), ([], []), []
    xs_all = [xp, xs]
    for l in range(DEPTH):
        i = l // 2
        h_all = [rmsnorm(x, norm_mix[l], BF16) for x in xs_all]
        if l % 2 == 0:
            q_all, k_all, v_all = (matmul(h_all[0], w_in_even, layer=i, col0=c * SB_WIDTH, n=SB_WIDTH,
                                          rider=h_all[1]) for c in range(3))
            zm_all = matmul(h_all[0], w_in_even, layer=i, col0=3 * SB_WIDTH, n=3 * RW_WIDTH, rider=h_all[1])
            w_lora = _pad_segs(w_in_even[i][:, 3 * SB_WIDTH + 3 * RW_WIDTH:])
            zl_all = matmul(h_all[0], w_lora, rider=h_all[1])
            mixed = []
            w2h, w2l = _split_bf16(_pad_rows(w2[i], 128))
            a2h, a2l = _split_bf16(_pad_rows(a2[i], 128))
            g2h, g2l = _split_bf16(_pad_rows(g2[i], 256))
            p = dict(mu=_pad_cols(mu_shift[i][None, :], _LORA_SEGS), w0=w0[i][None, :], a0=a0[i][None, :],
                     k_k=k_k[i][None, :], k_a=k_a[i][None, :], r_k=r_k[i].reshape(1, RW_WIDTH),
                     w2h=w2h, w2l=w2l, a2h=a2h, a2l=a2l, g2h=g2h, g2l=g2l, ones_bd=ones_bd,
                     lnx_w=lnx_w[i][None, :], lnx_b=lnx_b[i][None, :])
            for gi, (nb, nt) in enumerate(groups):
                q_rows, k_rows, v_rows, zb = q_all[gi], k_all[gi], v_all[gi], (zm_all[gi], zl_all[gi])
                if gi == 0:
                    att = sb_prompt(q_rows, k_rows, v_rows, sb_bias[i], nb, nt)
                    z_prev = jnp.zeros((nb, RW_PAD), F32)
                    wkv0 = jnp.zeros((nb, RW_HEADS, RW_HEAD_DIM, RW_HEAD_DIM), F32)
                else:
                    att = sb_sample(q_rows, k_rows, v_rows, sb_bias[i], cache_k, cache_v, i, page_table)
                    z_prev = _pad_cols(state_shift[i], _LORA_SEGS)
                    wkv0 = state_wkv[i]
                rw, s_fin = rwkv7(zb, z_prev, wkv0, p, nb, nt)
                mixed.append([att, rw])
                k_out[gi].append(k_rows.reshape(nb, nt, SB_HEADS, SB_HEAD_DIM))
                v_out[gi].append(v_rows.reshape(nb, nt, SB_HEADS, SB_HEAD_DIM))
                last_row = jnp.concatenate([zb[0].reshape(nb, nt, -1)[:, -1], zb[1].reshape(nb, nt, -1)[:, -1]],
                                           axis=-1)
                sh_out[gi].append(_unpad_cols(last_row))
                wkv_out[gi].append(s_fin)
            xs_all = list(matmul(mixed[0], w_out_even, layer=i, res=xs_all[0], rider=mixed[1],
                                 rider_res=xs_all[1]))
        else:
            lnw = ln_v_w[i][None, :]
            lnb = ln_v_b[i][None, :]
            z_all = matmul(h_all[0], w_in_odd, layer=i, act="gelu", rider=h_all[1], out_dtype=BF16)
            mixed = []
            for gi, (nb, nt) in enumerate(groups):
                z = z_all[gi]
                if gi == 0:
                    gated, _ = gmlp_gate(z, lnw, lnb, w_spatial[i], b_spatial[i].T, CHUNK, CHUNK, False)
                else:
                    eye = jnp.eye(nb, dtype=F32)
                    ws_s = w_spatial[i][:, :nt, :nt]
                    ws_bd = (eye[None, :, None, :, None] * ws_s[:, None, :, None, :]).reshape(
                        C_GROUPS, nb * nt, nb * nt)
                    bs_t = jnp.tile(b_spatial[i][:, :nt].T, (nb, 1))
                    gated, v_rows = gmlp_gate(z, lnw, lnb, ws_bd, bs_t, nb * nt, nt, True)
                    chunkv_out.append(v_rows.reshape(nb, nt, C_WIDTH))
                mixed.append(gated)
            xs_all = list(matmul(mixed[0], w_out_odd, layer=i, res=xs_all[0], rider=mixed[1],
                                 rider_res=xs_all[1]))
        hidden = []
        for gi, (nb, nt) in enumerate(groups):
            hf = rmsnorm(xs_all[gi], norm_ffn[l], BF16)
            if gi == 0:
                prev = jnp.zeros((nb, CONV_W - 1, 2 * FFN_DIM), F32)
                hid, up_tail = ffn_up_conv(hf, w_up, l, prev, conv_w[l], conv_b[l][None, :], nb, nt)
            else:
                prev = state_conv[l]
                up = matmul(hf, w_up, layer=l)
                hid = conv_gate(up, prev, conv_w[l], conv_b[l][None, :], nb, nt)
                up_tail = up.reshape(nb, nt, 2 * FFN_DIM)
            hidden.append(hid)
            cv_out[gi].append(jnp.concatenate([prev, up_tail], axis=1)[:, -(CONV_W - 1):])
        xs_all = list(matmul(hidden[0], w_down, layer=l, res=xs_all[0], rider=hidden[1], rider_res=xs_all[1]))
    y_prompt = rmsnorm(xs_all[0], norm_final, F32).reshape(bp, sp, D_MODEL)
    y_sample = rmsnorm(xs_all[1], norm_final, F32).reshape(bs, ts, D_MODEL)
    st = jnp.stack
    return (y_prompt, y_sample, st(k_out[0]), st(v_out[0]), st(k_out[1]), st(v_out[1]),
            st(sh_out[0]), st(sh_out[1]), st(wkv_out[0]), st(wkv_out[1]),
            st(cv_out[0]), st(cv_out[1]), st(chunkv_out))
```
